```python
import jax, jax.numpy as jnp
from jax import lax
import numpy as np

D_MODEL = 1024
BATCH = 8
SEQ = 2048
DEPTH = 1
DEC_BATCH = 32
DEC_SEQ = 4
PAST_LEN = 8192
PAGE_SIZE = 128

LRU_WIDTH = D_MODEL // 2
LRU_BLOCKS = 8
LRU_BLOCK_W = LRU_WIDTH // LRU_BLOCKS
CONV_W = 4
RG_C = 8.0
ATTN_WIDTH = D_MODEL - LRU_WIDTH
HEAD_DIM = 64
N_HEADS = ATTN_WIDTH // HEAD_DIM
IDX_HEADS = 8
IDX_DIM = 64
TOPK_MAX = 256
QBLOCK = 128
D_FF = 4 * D_MODEL
ROPE_THETA = 10000.0
EPS = 1e-6
PROJ_SIZES = (LRU_WIDTH, LRU_WIDTH, ATTN_WIDTH, ATTN_WIDTH, ATTN_WIDTH, IDX_HEADS * IDX_DIM, IDX_DIM, IDX_HEADS)
D_IN_PROJ = sum(PROJ_SIZES)

kernel_name = 'hymba_rglru_dsa_step'


def _rmsnorm(x, g):
    xf = x.astype(jnp.float32)
    y = xf * lax.rsqrt(jnp.mean(xf * xf, axis=-1, keepdims=True) + EPS)
    return (y * g.astype(jnp.float32)).astype(x.dtype)


def _rope(x, pos):
    half = x.shape[-1] // 2
    freq = ROPE_THETA ** (-jnp.arange(half, dtype=jnp.float32) / half)
    ang = pos.astype(jnp.float32)[:, None] * freq[None, :]
    cos = jnp.cos(ang)[:, None, :]
    sin = jnp.sin(ang)[:, None, :]
    xf = x.astype(jnp.float32)
    x1, x2 = xf[..., :half], xf[..., half:]
    return jnp.concatenate([x1 * cos - x2 * sin, x2 * cos + x1 * sin], axis=-1).astype(x.dtype)


def _split_proj(z):
    out = []
    s = 0
    for n in PROJ_SIZES:
        out.append(z[..., s:s + n])
        s += n
    return out


def _rglru(xb, conv_buf, h0, conv_w, conv_b, w_a, b_a, w_i, b_i, lam):
    B, T, W = xb.shape
    xp = jnp.concatenate([conv_buf.astype(xb.dtype), xb], axis=1)
    xc = conv_b + xp[:, 0:T] * conv_w[0]
    for j in range(1, CONV_W):
        xc = xc + xp[:, j:j + T] * conv_w[j]
    new_buf = xp[:, T:]
    xblk = xc.reshape(B, T, LRU_BLOCKS, LRU_BLOCK_W)
    r = jax.nn.sigmoid(jnp.einsum('btnc,ncd->btnd', xblk, w_a).reshape(B, T, W) + b_a)
    i = jax.nn.sigmoid(jnp.einsum('btnc,ncd->btnd', xblk, w_i).reshape(B, T, W) + b_i)
    log_a = -RG_C * jax.nn.softplus(-lam.astype(jnp.float32)) * r.astype(jnp.float32)
    a = jnp.exp(log_a)
    u = jnp.sqrt(-jnp.expm1(2.0 * log_a)) * (i * xc).astype(jnp.float32)

    def step(h, au):
        a_t, u_t = au
        h = a_t * h + u_t
        return h, h

    hT, hs = lax.scan(step, h0.astype(jnp.float32), (jnp.swapaxes(a, 0, 1), jnp.swapaxes(u, 0, 1)))
    return jnp.swapaxes(hs, 0, 1).astype(xb.dtype), hT.astype(xb.dtype), new_buf


def _index_scores(qi, wi, ki):
    dots = jnp.einsum('...thd,...sd->...ths', qi.astype(jnp.float32), ki.astype(jnp.float32)) * (IDX_DIM ** -0.5)
    return jnp.einsum('...ths,...th->...ts', jax.nn.relu(dots), wi.astype(jnp.float32)) * (IDX_HEADS ** -0.5)


def _sparse_attend(q, kg, vg, valid):
    s = jnp.einsum('...thd,...tkhd->...thk', q.astype(jnp.float32), kg.astype(jnp.float32)) * (HEAD_DIM ** -0.5)
    s = jnp.where(valid[..., :, None, :], s, -jnp.inf)
    p = jax.nn.softmax(s, axis=-1)
    return jnp.einsum('...thk,...tkhd->...thd', p.astype(vg.dtype), vg)


def _dsa_prompt(q, k, v, qi, ki, wi):
    B, S = q.shape[0], q.shape[1]
    nb = S // QBLOCK
    kk = min(TOPK_MAX, S // 4)
    qb = q.reshape(B, nb, QBLOCK, N_HEADS, HEAD_DIM)
    qib = qi.reshape(B, nb, QBLOCK, IDX_HEADS, IDX_DIM)
    wib = wi.reshape(B, nb, QBLOCK, IDX_HEADS)
    kpos = jnp.arange(S)

    def block(n):
        b = n // nb
        j = n % nb
        qpos = j * QBLOCK + jnp.arange(QBLOCK)
        sc = _index_scores(qib[b, j], wib[b, j], ki[b])
        sc = jnp.where(kpos[None, :] <= qpos[:, None], sc, -jnp.inf)
        _, idx = lax.top_k(sc, kk)
        valid = idx <= qpos[:, None]
        return _sparse_attend(qb[b, j], k[b][idx], v[b][idx], valid)

    o = lax.map(block, jnp.arange(B * nb))
    return o.reshape(B, S, N_HEADS, HEAD_DIM)


def _dsa_sample(q, k_new, v_new, qi, ki_new, wi, cache_k, cache_v, cache_kidx, page_table):
    DB, T = q.shape[0], q.shape[1]
    L = PAST_LEN + T
    kk = min(TOPK_MAX, L // 4)
    ki_past = cache_kidx[page_table].reshape(DB, PAST_LEN, IDX_DIM)
    ki_all = jnp.concatenate([ki_past.astype(ki_new.dtype), ki_new], axis=1)
    qpos = PAST_LEN + jnp.arange(T)
    kpos = jnp.arange(L)
    sc = _index_scores(qi, wi, ki_all)
    sc = jnp.where(kpos[None, None, :] <= qpos[None, :, None], sc, -jnp.inf)
    _, idx = lax.top_k(sc, kk)
    valid = idx <= qpos[None, :, None]
    is_past = (idx < PAST_LEN)[..., None, None]
    pidx = jnp.minimum(idx, PAST_LEN - 1)
    bidx = jnp.arange(DB)[:, None, None]
    phys = page_table[bidx, pidx // PAGE_SIZE]
    off = pidx % PAGE_SIZE
    nidx = jnp.clip(idx - PAST_LEN, 0, T - 1)
    kg = jnp.where(is_past, cache_k[phys, off].astype(q.dtype), k_new[bidx, nidx])
    vg = jnp.where(is_past, cache_v[phys, off].astype(q.dtype), v_new[bidx, nidx])
    return _sparse_attend(q, kg, vg, valid)


def _layer(x, pos, conv_buf, h0, attend, g_pre_mix, w_in, conv_w, conv_b, w_a, b_a, w_i, b_i, lam,
           w_out, g_post_mix, g_pre_ffn, w_ff1, w_ff2, g_post_ffn):
    B, T, _ = x.shape
    hn = _rmsnorm(x, g_pre_mix)
    xl, gate, q, k, v, qi, ki, wi = _split_proj(hn @ w_in)
    q = _rope(q.reshape(B, T, N_HEADS, HEAD_DIM), pos)
    k = _rope(k.reshape(B, T, N_HEADS, HEAD_DIM), pos)
    v = v.reshape(B, T, N_HEADS, HEAD_DIM)
    qi = _rope(qi.reshape(B, T, IDX_HEADS, IDX_DIM), pos)
    ki = _rope(ki[:, :, None, :], pos)[:, :, 0, :]
    lru, hT, new_buf = _rglru(xl, conv_buf, h0, conv_w, conv_b, w_a, b_a, w_i, b_i, lam)
    att = attend(q, k, v, qi, ki, wi)
    mix = jnp.concatenate([lru * jax.nn.gelu(gate), att.reshape(B, T, ATTN_WIDTH)], axis=-1) @ w_out
    x = x + _rmsnorm(mix, g_post_mix)
    f = jnp.square(jax.nn.relu(_rmsnorm(x, g_pre_ffn) @ w_ff1)) @ w_ff2
    x = x + _rmsnorm(f, g_post_ffn)
    return x, (k, v, ki, hT, new_buf)


def setup_inputs(seed: int = 0) -> dict:
    key = jax.random.key(seed)
    ks = jax.random.split(key, 24)
    n_pages = PAST_LEN // PAGE_SIZE
    n_pool = (DEC_BATCH * n_pages * 5) // 4
    nrm = lambda k, shape, s: jax.random.normal(k, shape, jnp.float32) * s
    page_table = jax.random.permutation(ks[0], n_pool)[:DEC_BATCH * n_pages].reshape(DEC_BATCH, n_pages).astype(jnp.int32)
    u = jax.random.uniform(ks[1], (DEPTH, LRU_WIDTH), jnp.float32, 0.9, 0.999)
    a0 = u ** (1.0 / RG_C)
    lam = jnp.log(a0 / (1.0 - a0))
    return {
        'x_prompt': nrm(ks[2], (BATCH, SEQ, D_MODEL), 1.0),
        'x_sample': nrm(ks[3], (DEC_BATCH, DEC_SEQ, D_MODEL), 1.0),
        'cache_k': nrm(ks[4], (DEPTH, n_pool, PAGE_SIZE, N_HEADS, HEAD_DIM), 1.0),
        'cache_v': nrm(ks[5], (DEPTH, n_pool, PAGE_SIZE, N_HEADS, HEAD_DIM), 1.0),
        'cache_kidx': nrm(ks[6], (DEPTH, n_pool, PAGE_SIZE, IDX_DIM), 1.0),
        'state_h': nrm(ks[7], (DEPTH, DEC_BATCH, LRU_WIDTH), 0.5),
        'state_conv': nrm(ks[8], (DEPTH, DEC_BATCH, CONV_W - 1, LRU_WIDTH), 1.0),
        'page_table': page_table,
        'g_pre_mix': 1.0 + nrm(ks[9], (DEPTH, D_MODEL), 0.02),
        'w_in': nrm(ks[10], (DEPTH, D_MODEL, D_IN_PROJ), D_MODEL ** -0.5),
        'conv_w': nrm(ks[11], (DEPTH, CONV_W, LRU_WIDTH), CONV_W ** -0.5),
        'conv_b': nrm(ks[12], (DEPTH, LRU_WIDTH), 0.01),
        'w_a': nrm(ks[13], (DEPTH, LRU_BLOCKS, LRU_BLOCK_W, LRU_BLOCK_W), LRU_BLOCK_W ** -0.5),
        'b_a': nrm(ks[14], (DEPTH, LRU_WIDTH), 0.01),
        'w_i': nrm(ks[15], (DEPTH, LRU_BLOCKS, LRU_BLOCK_W, LRU_BLOCK_W), LRU_BLOCK_W ** -0.5),
        'b_i': nrm(ks[16], (DEPTH, LRU_WIDTH), 0.01),
        'lam': lam,
        'w_out': nrm(ks[17], (DEPTH, D_MODEL, D_MODEL), D_MODEL ** -0.5),
        'g_post_mix': 1.0 + nrm(ks[18], (DEPTH, D_MODEL), 0.02),
        'g_pre_ffn': 1.0 + nrm(ks[19], (DEPTH, D_MODEL), 0.02),
        'w_ff1': nrm(ks[20], (DEPTH, D_MODEL, D_FF), D_MODEL ** -0.5),
        'w_ff2': nrm(ks[21], (DEPTH, D_FF, D_MODEL), D_FF ** -0.5),
        'g_post_ffn': 1.0 + nrm(ks[22], (DEPTH, D_MODEL), 0.02),
    }


def reference(x_prompt, x_sample, cache_k, cache_v, cache_kidx, state_h, state_conv, page_table,
              g_pre_mix, w_in, conv_w, conv_b, w_a, b_a, w_i, b_i, lam, w_out, g_post_mix,
              g_pre_ffn, w_ff1, w_ff2, g_post_ffn):
    B, S, _ = x_prompt.shape
    DB, T, _ = x_sample.shape
    pos_p = jnp.arange(S)
    pos_s = PAST_LEN + jnp.arange(T)
    xp, xs = x_prompt, x_sample
    p_states, s_states = [], []
    for l in range(DEPTH):
        lw = (g_pre_mix[l], w_in[l], conv_w[l], conv_b[l], w_a[l], b_a[l], w_i[l], b_i[l], lam[l],
              w_out[l], g_post_mix[l], g_pre_ffn[l], w_ff1[l], w_ff2[l], g_post_ffn[l])
        xp, sp = _layer(xp, pos_p, jnp.zeros((B, CONV_W - 1, LRU_WIDTH), xp.dtype),
                        jnp.zeros((B, LRU_WIDTH), xp.dtype), _dsa_prompt, *lw)
        ck, cv, ci = cache_k[l], cache_v[l], cache_kidx[l]
        attend_s = lambda q, k, v, qi, ki, wi: _dsa_sample(q, k, v, qi, ki, wi, ck, cv, ci, page_table)
        xs, ss = _layer(xs, pos_s, state_conv[l], state_h[l], attend_s, *lw)
        p_states.append(sp)
        s_states.append(ss)
    k_prompt = jnp.stack([s[0] for s in p_states])
    v_prompt = jnp.stack([s[1] for s in p_states])
    kidx_prompt = jnp.stack([s[2] for s in p_states])
    h_prompt = jnp.stack([s[3] for s in p_states])
    conv_prompt = jnp.stack([s[4] for s in p_states])
    k_sample = jnp.stack([s[0] for s in s_states])
    v_sample = jnp.stack([s[1] for s in s_states])
    kidx_sample = jnp.stack([s[2] for s in s_states])
    h_sample = jnp.stack([s[3] for s in s_states])
    conv_sample = jnp.stack([s[4] for s in s_states])
    return (xp, xs, k_prompt, v_prompt, kidx_prompt, h_prompt, conv_prompt,
            k_sample, v_sample, kidx_sample, h_sample, conv_sample)
```

```python
import functools

import jax
import jax.numpy as jnp
from jax import lax
from jax.experimental import pallas as pl
from jax.experimental.pallas import tpu as pltpu

LRU_BLOCKS = 8
CONV_W = 4
RG_C = 8.0
HEAD_DIM = 64
IDX_HEADS = 8
IDX_DIM = 64
TOPK_MAX = 256
QBLOCK = 128
ROPE_THETA = 10000.0
EPS = 1e-6

LANES = 128
SUBLANES = 8
VMEM_LIMIT_BYTES = 48 * 1024 * 1024

_F32 = jnp.float32
_BF16 = jnp.bfloat16
_INT_MIN = -(2 ** 31)
_NT = (((1,), (1,)), ((), ()))


def _rmsnorm(x, g):
    return x * lax.rsqrt(jnp.mean(x * x, axis=-1, keepdims=True) + EPS) * g


def _swap_halves(x):
    w = x.shape[-1]
    lane = lax.broadcasted_iota(jnp.int32, x.shape, 1)
    fwd = pltpu.roll(x, w - HEAD_DIM // 2, 1)
    bwd = pltpu.roll(x, HEAD_DIM // 2, 1)
    return jnp.where((lane & (HEAD_DIM // 2)) == 0, fwd, bwd)


def _rope(x, cos, sin):
    return x * cos + _swap_halves(x) * sin


def _tile_lanes(t, n):
    return jnp.concatenate([t] * n, axis=1) if n > 1 else t


def _project(x, cos, sin, g_ref, win_ref, width):
    hn = _rmsnorm(x, g_ref[...]).astype(_BF16)
    z = jnp.dot(hn, win_ref[...], preferred_element_type=_F32)
    w = width
    n = w // LANES
    cos_w, sin_w = _tile_lanes(cos, n), _tile_lanes(sin, n)
    out = {
        "xl": z[:, 0:w],
        "gate": z[:, w:2 * w],
        "q": _rope(z[:, 2 * w:3 * w], cos_w, sin_w),
        "k": _rope(z[:, 3 * w:4 * w], cos_w, sin_w),
        "v": z[:, 4 * w:5 * w],
        "qi": _rope(z[:, 5 * w:6 * w], cos_w, sin_w),
        "kw": z[:, 6 * w:6 * w + LANES],
    }
    out["kir"] = _rope(out["kw"], cos, sin)
    return out


def _expm1(x):
    u = jnp.exp(x)
    d = u - 1.0
    return jnp.where(u == 1.0, x, jnp.where(d == -1.0, -1.0, d * x / jnp.log(u)))


def _lru_gates(xc, wg_ref, ba, bi, lam):
    w = xc.shape[1]
    g = jnp.dot(xc.astype(_BF16), wg_ref[...], preferred_element_type=_F32)
    r = jax.nn.sigmoid(g[:, :w] + ba)
    i = jax.nn.sigmoid(g[:, w:] + bi)
    log_a = (-RG_C * jax.nn.softplus(-lam)) * r
    a = jnp.exp(log_a)
    u = jnp.sqrt(-_expm1(2.0 * log_a)) * (i * xc)
    return a, u


def _scan_rows(a, u):
    tm, w = a.shape
    row = lax.broadcasted_iota(jnp.int32, a.shape, 0)
    s = 1
    while s < tm:
        if s < SUBLANES:
            keep = row >= s
            a_sh = jnp.where(keep, pltpu.roll(a, s, 0), 1.0)
            u_sh = jnp.where(keep, pltpu.roll(u, s, 0), 0.0)
        else:
            a_sh = jnp.concatenate([jnp.ones((s, w), _F32), a[:tm - s]], axis=0)
            u_sh = jnp.concatenate([jnp.zeros((s, w), _F32), u[:tm - s]], axis=0)
        u = a * u_sh + u
        a = a * a_sh
        s *= 2
    return a, u


def _prompt_proj_kernel(x_ref, cos_ref, sin_ref, g_ref, win_ref, convw_ref, convb_ref, wg_ref,
                        ba_ref, bi_ref, lam_ref,
                        k_ref, v_ref, kir_ref, kw_ref, q_ref, qi_ref, kt_ref, kit_ref, vb_ref,
                        lru_ref, h_ref, conv_ref, hcar_ref, ccar_ref, *, width):
    t = pl.program_id(1)

    @pl.when(t == 0)
    def _():
        hcar_ref[...] = jnp.zeros_like(hcar_ref)
        ccar_ref[...] = jnp.zeros_like(ccar_ref)

    x = x_ref[0]
    tm = x.shape[0]
    p = _project(x, cos_ref[...], sin_ref[...], g_ref, win_ref, width)
    scale = HEAD_DIM ** -0.5
    k_ref[0] = p["k"]
    v_ref[0] = p["v"]
    kir_ref[0] = p["kir"]
    kw_ref[0] = p["kw"]
    q_ref[0] = (p["q"] * scale).astype(_BF16)
    qi_ref[0] = (p["qi"] * (IDX_DIM ** -0.5)).astype(_BF16)
    kt_ref[0] = p["k"].T.astype(_BF16)
    ki = p["kir"][:, :IDX_DIM]
    kit_ref[0] = jnp.concatenate([ki, ki], axis=1).T.astype(_BF16)
    vb_ref[0] = p["v"].astype(_BF16)

    xl = p["xl"]
    ext = jnp.concatenate([ccar_ref[...], xl], axis=0)
    xc = convb_ref[...]
    for j in range(CONV_W):
        lo = SUBLANES - (CONV_W - 1) + j
        xc = xc + ext[lo:lo + tm] * convw_ref[j:j + 1, :]
    ccar_ref[...] = xl[tm - SUBLANES:]
    conv_ref[0] = xl[tm - SUBLANES:]

    a, u = _lru_gates(xc, wg_ref, ba_ref[...], bi_ref[...], lam_ref[...])
    a_cum, u_cum = _scan_rows(a, u)
    hs = a_cum * hcar_ref[0:1, :] + u_cum
    h_last = jnp.broadcast_to(hs[tm - 1:tm, :], (SUBLANES, width))
    hcar_ref[...] = h_last
    h_ref[0] = h_last
    lru_ref[0] = (hs * jax.nn.gelu(p["gate"])).astype(_BF16)


def _prompt_proj(x, cos, sin, g_pre, w_in, conv_w, conv_b, w_gate, b_a, b_i, lam, *, tile):
    b, t, d = x.shape
    w = conv_b.shape[-1]
    nt = t // tile
    const = lambda shape: pl.BlockSpec(shape, lambda bi, ti: (0,) * len(shape))
    tok = lambda last: pl.BlockSpec((1, tile, last), lambda bi, ti: (bi, ti, 0))
    state = pl.BlockSpec((1, SUBLANES, w), lambda bi, ti: (bi, 0, 0))
    out_shape = (
        jax.ShapeDtypeStruct((b, t, w), _F32),
        jax.ShapeDtypeStruct((b, t, w), _F32),
        jax.ShapeDtypeStruct((b, t, LANES), _F32),
        jax.ShapeDtypeStruct((b, t, LANES), _F32),
        jax.ShapeDtypeStruct((b, t, w), _BF16),
        jax.ShapeDtypeStruct((b, t, w), _BF16),
        jax.ShapeDtypeStruct((b, w, t), _BF16),
        jax.ShapeDtypeStruct((b, LANES, t), _BF16),
        jax.ShapeDtypeStruct((b, t, w), _BF16),
        jax.ShapeDtypeStruct((b, t, w), _BF16),
        jax.ShapeDtypeStruct((b, SUBLANES, w), _F32),
        jax.ShapeDtypeStruct((b, SUBLANES, w), _F32),
    )
    out_specs = (
        tok(w), tok(w), tok(LANES), tok(LANES), tok(w), tok(w),
        pl.BlockSpec((1, w, tile), lambda bi, ti: (bi, 0, ti)),
        pl.BlockSpec((1, LANES, tile), lambda bi, ti: (bi, 0, ti)),
        tok(w), tok(w), state, state,
    )
    in_specs = [
        tok(d),
        pl.BlockSpec((tile, LANES), lambda bi, ti: (ti, 0)),
        pl.BlockSpec((tile, LANES), lambda bi, ti: (ti, 0)),
        const((1, d)), const(w_in.shape), const(conv_w.shape), const((1, w)), const(w_gate.shape),
        const((1, w)), const((1, w)), const((1, w)),
    ]
    return pl.pallas_call(
        functools.partial(_prompt_proj_kernel, width=w),
        grid=(b, nt),
        in_specs=in_specs,
        out_specs=out_specs,
        out_shape=out_shape,
        scratch_shapes=[pltpu.VMEM((SUBLANES, w), _F32), pltpu.VMEM((SUBLANES, w), _F32)],
        compiler_params=pltpu.CompilerParams(
            dimension_semantics=("arbitrary", "arbitrary"), vmem_limit_bytes=VMEM_LIMIT_BYTES),
        name="prompt_proj",
    )(x, cos, sin, g_pre, w_in, conv_w, conv_b, w_gate, b_a, b_i, lam)


def _sortable_key(score, admissible):
    score = jnp.where(score == 0.0, 0.0, score)
    bits = pltpu.bitcast(score, jnp.int32)
    key = bits ^ ((bits >> 31) & 0x7FFFFFFF)
    return jnp.where(admissible, key, _INT_MIN)


def _topk_bias(key_ref, bias_ref, jcut_ref, *, ncols, topk):
    rows = key_ref.shape[0]
    nchunk = ncols // LANES
    lane = lax.broadcasted_iota(jnp.int32, (rows, LANES), 1)
    kf = float(topk)

    def count(pred):
        acc = jnp.zeros((rows, LANES), _F32)
        for c in range(nchunk):
            acc = acc + pred(key_ref[:, c * LANES:(c + 1) * LANES], c)
        return jnp.broadcast_to(jnp.sum(acc, axis=1, keepdims=True), (rows, LANES))

    def thr_step(i, thr):
        cand = thr + jnp.left_shift(jnp.int32(1), 31 - i)
        cnt = count(lambda k, c: jnp.where(k >= cand, 1.0, 0.0))
        return jnp.where(cnt >= kf, cand, thr)

    thr = lax.fori_loop(0, 32, thr_step, jnp.full((rows, LANES), _INT_MIN, jnp.int32))

    need = kf - count(lambda k, c: jnp.where(k > thr, 1.0, 0.0))
    n_eq = count(lambda k, c: jnp.where(k == thr, 1.0, 0.0))
    jcut_ref[...] = jnp.full((rows, LANES), ncols, jnp.int32)

    @pl.when(jnp.max(n_eq - need) > 0.0)
    def _():
        nbits = max(1, (ncols - 1).bit_length())

        def cut_step(i, cut):
            cand = cut + jnp.left_shift(jnp.int32(1), nbits - 1 - i)
            below = count(lambda k, c: jnp.where(
                k == thr, jnp.where(c * LANES + lane < cand, 1.0, 0.0), 0.0))
            return jnp.where(below < need, cand, cut)

        cut = lax.fori_loop(0, nbits, cut_step, jnp.zeros((rows, LANES), jnp.int32))
        jcut_ref[...] = cut + 1

    jcut = jnp.where(thr == _INT_MIN, 0, jcut_ref[...])
    for c in range(nchunk):
        k = key_ref[:, c * LANES:(c + 1) * LANES]
        tie = jnp.where(c * LANES + lane < jcut, 0.0, -jnp.inf)
        bias_ref[:, c * LANES:(c + 1) * LANES] = jnp.where(
            k > thr, 0.0, jnp.where(k == thr, tie, -jnp.inf))


def _prompt_attn_kernel(qi_ref, q_ref, kw_ref, kit_ref, kt_ref, v_ref, o_ref,
                        key_ref, bias_ref, jcut_ref, wib_ref, *, ncols, jbase, topk):
    j = jbase + pl.program_id(1)
    qb = q_ref.shape[1]
    width = q_ref.shape[2]
    heads = width // HEAD_DIM
    lane_w = lax.broadcasted_iota(jnp.int32, (qb, width), 1)
    even = (lane_w & HEAD_DIM) == 0

    kw = kw_ref[0]
    for h in range(IDX_HEADS):
        col = IDX_DIM + h
        wib_ref[h] = jnp.broadcast_to(kw[:, col:col + 1], (qb, LANES))

    qi = qi_ref[0]
    zero = jnp.zeros_like(qi)
    qi_par = (jnp.where(even, qi, zero), jnp.where(even, zero, qi))
    cw = 2 * LANES
    qpos = j * qb + lax.broadcasted_iota(jnp.int32, (qb, cw), 0)
    for c in range(ncols // cw):
        kic = kit_ref[0, :, c * cw:(c + 1) * cw]
        acc = jnp.zeros((qb, cw), _F32)
        for h in range(IDX_HEADS):
            lhs = qi_par[h % 2][:, (h // 2) * LANES:(h // 2 + 1) * LANES]
            d = jnp.dot(lhs, kic, preferred_element_type=_F32)
            acc = acc + jnp.maximum(d, 0.0) * _tile_lanes(wib_ref[h], cw // LANES)
        score = acc * (IDX_HEADS ** -0.5)
        col = c * cw + lax.broadcasted_iota(jnp.int32, (qb, cw), 1)
        key_ref[:, c * cw:(c + 1) * cw] = _sortable_key(score, col <= qpos)

    _topk_bias(key_ref, bias_ref, jcut_ref, ncols=ncols, topk=topk)

    q = q_ref[0]
    zero = jnp.zeros_like(q)
    q_par = (jnp.where(even, q, zero), jnp.where(even, zero, q))
    lane = lax.broadcasted_iota(jnp.int32, (qb, LANES), 1)
    for g in range(heads // 2):
        sl = slice(g * LANES, (g + 1) * LANES)
        kt = kt_ref[0, sl, :]
        vv = v_ref[0, :, sl]
        outs = []
        for par in range(2):
            s = jnp.dot(q_par[par][:, sl], kt, preferred_element_type=_F32) + bias_ref[...]
            m = jnp.max(s, axis=1, keepdims=True)
            p = jnp.exp(s - m)
            l = jnp.sum(p, axis=1, keepdims=True)
            o = jnp.dot(p.astype(_BF16), vv, preferred_element_type=_F32)
            outs.append(o * (1.0 / l))
        o_ref[0, :, sl] = jnp.where(lane < HEAD_DIM, outs[0], outs[1]).astype(o_ref.dtype)


def _prompt_attn(qi, q, kw, kit, kt, vb, *, ncols, jbase, nblocks, topk):
    b, t, w = q.shape
    qblk = lambda last: pl.BlockSpec((1, QBLOCK, last), lambda bi, ji: (bi, jbase + ji, 0))
    return pl.pallas_call(
        functools.partial(_prompt_attn_kernel, ncols=ncols, jbase=jbase, topk=topk),
        grid=(b, nblocks),
        in_specs=[
            qblk(w), qblk(w), qblk(LANES),
            pl.BlockSpec((1, LANES, ncols), lambda bi, ji: (bi, 0, 0)),
            pl.BlockSpec((1, w, ncols), lambda bi, ji: (bi, 0, 0)),
            pl.BlockSpec((1, ncols, w), lambda bi, ji: (bi, 0, 0)),
        ],
        out_specs=pl.BlockSpec((1, QBLOCK, w), lambda bi, ji: (bi, ji, 0)),
        out_shape=jax.ShapeDtypeStruct((b, nblocks * QBLOCK, w), _BF16),
        scratch_shapes=[
            pltpu.VMEM((QBLOCK, ncols), jnp.int32),
            pltpu.VMEM((QBLOCK, ncols), _F32),
            pltpu.VMEM((QBLOCK, LANES), jnp.int32),
            pltpu.VMEM((IDX_HEADS, QBLOCK, LANES), _F32),
        ],
        compiler_params=pltpu.CompilerParams(
            dimension_semantics=("arbitrary", "arbitrary"), vmem_limit_bytes=VMEM_LIMIT_BYTES),
        name=f"prompt_attn_{ncols}",
    )(qi, q, kw, kit, kt, vb)


def _mlp_kernel(x_ref, lru_ref, att_ref, wout_ref, gpm_ref, gpf_ref, w1_ref, w2_ref, gpo_ref, y_ref):
    w = lru_ref.shape[1]
    mix = (jnp.dot(lru_ref[...], wout_ref[0:w, :], preferred_element_type=_F32)
           + jnp.dot(att_ref[...], wout_ref[w:, :], preferred_element_type=_F32))
    x1 = x_ref[...] + _rmsnorm(mix, gpm_ref[...])
    hn = _rmsnorm(x1, gpf_ref[...]).astype(_BF16)
    f = jnp.dot(hn, w1_ref[...], preferred_element_type=_F32)
    f = jnp.square(jnp.maximum(f, 0.0)).astype(_BF16)
    f2 = jnp.dot(f, w2_ref[...], preferred_element_type=_F32)
    y_ref[...] = x1 + _rmsnorm(f2, gpo_ref[...])


def _mlp(x, lru, att, w_out, g_post_mix, g_pre_ffn, w_ff1, w_ff2, g_post_ffn, *, tile):
    n, d = x.shape
    w = lru.shape[1]
    const = lambda shape: pl.BlockSpec(shape, lambda i: (0,) * len(shape),
                                       pipeline_mode=pl.Buffered(1))
    tok = lambda last: pl.BlockSpec((tile, last), lambda i: (i, 0))
    return pl.pallas_call(
        _mlp_kernel,
        grid=(n // tile,),
        in_specs=[tok(d), tok(w), tok(w), const(w_out.shape), const((1, d)), const((1, d)),
                  const(w_ff1.shape), const(w_ff2.shape), const((1, d))],
        out_specs=tok(d),
        out_shape=jax.ShapeDtypeStruct((n, d), _F32),
        compiler_params=pltpu.CompilerParams(
            dimension_semantics=("arbitrary",), vmem_limit_bytes=VMEM_LIMIT_BYTES),
        name="mlp",
    )(x, lru, att, w_out, g_post_mix, g_pre_ffn, w_ff1, w_ff2, g_post_ffn)


def _sample_proj_kernel(x_ref, cos_ref, sin_ref, g_ref, win_ref, convw_ref, convb_ref, wg_ref,
                        ba_ref, bi_ref, lam_ref, cst_ref, h0_ref,
                        k_ref, v_ref, kir_ref, kw_ref, q_ref, qi_ref, lru_ref, h_ref, cnew_ref,
                        *, width, nseq, nstep):
    p = _project(x_ref[...], cos_ref[...], sin_ref[...], g_ref, win_ref, width)
    k_ref[...] = p["k"]
    v_ref[...] = p["v"]
    kir_ref[...] = p["kir"]
    kw_ref[...] = p["kw"]
    q_ref[...] = (p["q"] * (HEAD_DIM ** -0.5)).astype(_BF16)
    qi_ref[...] = (p["qi"] * (IDX_DIM ** -0.5)).astype(_BF16)

    xl = p["xl"]
    n = nseq * nstep
    ext = jnp.concatenate([cst_ref[...], xl], axis=0)
    xc = convb_ref[...]
    for j in range(CONV_W):
        xc = xc + ext[j * nseq:j * nseq + n] * convw_ref[j:j + 1, :]
    cnew_ref[...] = ext[n:]

    a, u = _lru_gates(xc, wg_ref, ba_ref[...], bi_ref[...], lam_ref[...])
    h = h0_ref[...]
    hs = []
    for s in range(nstep):
        h = a[s * nseq:(s + 1) * nseq] * h + u[s * nseq:(s + 1) * nseq]
        hs.append(h)
    h_ref[...] = h
    lru_ref[...] = (jnp.concatenate(hs, axis=0) * jax.nn.gelu(p["gate"])).astype(_BF16)


def _sample_proj(x, cos, sin, g_pre, w_in, conv_w, conv_b, w_gate, b_a, b_i, lam, cst, h0,
                 *, nseq, nstep):
    n, d = x.shape
    w = conv_b.shape[-1]
    out_shape = (
        jax.ShapeDtypeStruct((n, w), _F32), jax.ShapeDtypeStruct((n, w), _F32),
        jax.ShapeDtypeStruct((n, LANES), _F32), jax.ShapeDtypeStruct((n, LANES), _F32),
        jax.ShapeDtypeStruct((n, w), _BF16), jax.ShapeDtypeStruct((n, w), _BF16),
        jax.ShapeDtypeStruct((n, w), _BF16),
        jax.ShapeDtypeStruct((nseq, w), _F32),
        jax.ShapeDtypeStruct(((CONV_W - 1) * nseq, w), _F32),
    )
    return pl.pallas_call(
        functools.partial(_sample_proj_kernel, width=w, nseq=nseq, nstep=nstep),
        out_shape=out_shape,
        compiler_params=pltpu.CompilerParams(vmem_limit_bytes=VMEM_LIMIT_BYTES),
        name="sample_proj",
    )(x, cos, sin, g_pre, w_in, conv_w, conv_b, w_gate, b_a, b_i, lam, cst, h0)


def _sample_index_kernel(pt_ref, qi_ref, wi_ref, kin_ref, cache_ref, sc_ref, buf_ref, sem_ref,
                         *, npages, page, nstep):
    b = pl.program_id(0)
    nb = pl.num_programs(0)
    slot = b % 2

    def page_copy(seq, p, sl):
        return pltpu.make_async_copy(cache_ref.at[pt_ref[seq, p]],
                                     buf_ref.at[sl, pl.ds(p * page, page)], sem_ref.at[sl])

    @pl.when(b == 0)
    def _():
        for p in range(npages):
            page_copy(0, p, 0).start()

    @pl.when(b + 1 < nb)
    def _():
        for p in range(npages):
            page_copy(b + 1, p, 1 - slot).start()

    for p in range(npages):
        page_copy(b, p, slot).wait()

    qi = qi_ref[0]
    wi = wi_ref[0]

    def scores(keys):
        n = keys.shape[0]
        d = lax.dot_general(qi, keys.astype(_BF16), _NT, preferred_element_type=_F32)
        wd = jnp.maximum(d, 0.0) * wi
        return jnp.sum(wd.reshape(nstep, IDX_HEADS, n), axis=1) * (IDX_HEADS ** -0.5)

    past = npages * page
    chunk = 16 * page
    for c in range(past // chunk):
        sc_ref[0, :, c * chunk:(c + 1) * chunk] = scores(buf_ref[slot, c * chunk:(c + 1) * chunk, :])
    sc_ref[0, :, past:] = scores(kin_ref[0])


def _sample_index(page_table, qi_rows, wi_rows, ki_new, cache_kidx, *, nstep):
    nseq, npages = page_table.shape
    page = cache_kidx.shape[1]
    past = npages * page
    ncols = past + LANES
    grid_spec = pltpu.PrefetchScalarGridSpec(
        num_scalar_prefetch=1,
        grid=(nseq,),
        in_specs=[
            pl.BlockSpec((1,) + qi_rows.shape[1:], lambda b, pt: (b, 0, 0)),
            pl.BlockSpec((1,) + wi_rows.shape[1:], lambda b, pt: (b, 0, 0)),
            pl.BlockSpec((1,) + ki_new.shape[1:], lambda b, pt: (b, 0, 0)),
            pl.BlockSpec(memory_space=pl.ANY),
        ],
        out_specs=pl.BlockSpec((1, nstep, ncols), lambda b, pt: (b, 0, 0)),
        scratch_shapes=[pltpu.VMEM((2, past, IDX_DIM), _F32), pltpu.SemaphoreType.DMA((2,))],
    )
    return pl.pallas_call(
        functools.partial(_sample_index_kernel, npages=npages, page=page, nstep=nstep),
        grid_spec=grid_spec,
        out_shape=jax.ShapeDtypeStruct((nseq, nstep, ncols), _F32),
        compiler_params=pltpu.CompilerParams(
            dimension_semantics=("arbitrary",), vmem_limit_bytes=VMEM_LIMIT_BYTES),
        name="sample_index",
    )(page_table, qi_rows, wi_rows, ki_new, cache_kidx)


def _sample_topk_kernel(sc_ref, bias_ref, key_ref, jcut_ref, *, past, nstep, topk):
    rows, ncols = sc_ref.shape
    cw = 4 * LANES
    step = lax.broadcasted_iota(jnp.int32, (rows, cw), 0) % nstep
    for c in range(ncols // cw + (1 if ncols % cw else 0)):
        lo = c * cw
        wdt = min(cw, ncols - lo)
        col = lo + lax.broadcasted_iota(jnp.int32, (rows, wdt), 1)
        key_ref[:, lo:lo + wdt] = _sortable_key(sc_ref[:, lo:lo + wdt], col <= past + step[:, :wdt])
    _topk_bias(key_ref, bias_ref, jcut_ref, ncols=ncols, topk=topk)


def _sample_topk(sc, *, past, nstep, topk):
    rows, ncols = sc.shape
    return pl.pallas_call(
        functools.partial(_sample_topk_kernel, past=past, nstep=nstep, topk=topk),
        out_shape=jax.ShapeDtypeStruct((rows, ncols), _F32),
        scratch_shapes=[pltpu.VMEM((rows, ncols), jnp.int32), pltpu.VMEM((rows, LANES), jnp.int32)],
        compiler_params=pltpu.CompilerParams(vmem_limit_bytes=VMEM_LIMIT_BYTES),
        name="sample_topk",
    )(sc)


def _sample_attn_kernel(pt_ref, q_ref, bias_ref, btail_ref, kn_ref, vn_ref, ck_ref, cv_ref, o_ref,
                        kbuf_ref, vbuf_ref, sem_ref, qbd_ref, m_ref, l_ref, acc_ref,
                        *, cpages, page, nstep):
    b = pl.program_id(0)
    c = pl.program_id(1)
    nchunks = pl.num_programs(1)
    step = b * nchunks + c
    total = pl.num_programs(0) * nchunks
    slot = step % 2
    width = q_ref.shape[2]
    heads = width // HEAD_DIM
    rows = nstep * heads

    def copies(seq, chunk, p, sl):
        phys = pt_ref[seq, chunk * cpages + p]
        dst = pl.ds(p * page, page)
        return (pltpu.make_async_copy(ck_ref.at[phys], kbuf_ref.at[sl, dst], sem_ref.at[0, sl]),
                pltpu.make_async_copy(cv_ref.at[phys], vbuf_ref.at[sl, dst], sem_ref.at[1, sl]))

    def fetch(seq, chunk, sl):
        for p in range(cpages):
            for cp in copies(seq, chunk, p, sl):
                cp.start()

    @pl.when(step == 0)
    def _():
        fetch(0, 0, 0)

    @pl.when(step + 1 < total)
    def _():
        nxt = step + 1
        fetch(nxt // nchunks, nxt % nchunks, 1 - slot)

    for p in range(cpages):
        for cp in copies(b, c, p, slot):
            cp.wait()

    row = lax.broadcasted_iota(jnp.int32, (rows, width), 0)
    lane = lax.broadcasted_iota(jnp.int32, (rows, width), 1)
    own_head = (row % heads) == (lane // HEAD_DIM)

    def expand(x):
        return jnp.broadcast_to(x[:, None, :], (nstep, heads, x.shape[1])).reshape(rows, x.shape[1])

    @pl.when(c == 0)
    def _():
        q = expand(q_ref[0].astype(_F32))
        qbd_ref[...] = jnp.where(own_head, q, 0.0).astype(_BF16)
        m_ref[...] = jnp.full(m_ref.shape, -1e30, _F32)
        l_ref[...] = jnp.zeros_like(l_ref)
        acc_ref[...] = jnp.zeros_like(acc_ref)

    s = lax.dot_general(qbd_ref[...], kbuf_ref[slot].astype(_BF16), _NT,
                        preferred_element_type=_F32)
    s = s + expand(bias_ref[0])
    m_old = m_ref[...]
    m_new = jnp.maximum(m_old, jnp.max(s, axis=1, keepdims=True))
    alpha = jnp.exp(m_old - m_new)
    pr = jnp.exp(s - m_new)
    l_ref[...] = alpha * l_ref[...] + jnp.sum(pr, axis=1, keepdims=True)
    acc_ref[...] = alpha * acc_ref[...] + jnp.dot(
        pr.astype(_BF16), vbuf_ref[slot].astype(_BF16), preferred_element_type=_F32)
    m_ref[...] = m_new

    @pl.when(c == nchunks - 1)
    def _():
        qf = qbd_ref[...].astype(_F32)
        kn = kn_ref[0].astype(_BF16).astype(_F32)
        vn = vn_ref[0].astype(_BF16).astype(_F32)
        bt = expand(btail_ref[0])
        sn = [jnp.sum(qf * kn[t:t + 1, :], axis=1, keepdims=True) + bt[:, t:t + 1]
              for t in range(nstep)]
        m_old = m_ref[...]
        m_new = m_old
        for t in range(nstep):
            m_new = jnp.maximum(m_new, sn[t])
        alpha = jnp.exp(m_old - m_new)
        l = alpha * l_ref[...]
        acc = alpha * acc_ref[...]
        for t in range(nstep):
            pt_ = jnp.exp(sn[t] - m_new)
            l = l + pt_
            acc = acc + pt_.astype(_BF16).astype(_F32) * vn[t:t + 1, :]
        o = jnp.where(own_head, acc * (1.0 / l), 0.0)
        o_ref[0] = jnp.sum(o.reshape(nstep, heads, width), axis=1).astype(o_ref.dtype)


def _sample_attn(page_table, q, bias, k_new, v_new, cache_k, cache_v, *, cpages):
    nseq, nstep, w = q.shape
    npages = page_table.shape[1]
    page = cache_k.shape[1]
    chunk = cpages * page
    nchunks = npages // cpages
    rows = nstep * (w // HEAD_DIM)
    tail_blk = (npages * page) // LANES
    grid_spec = pltpu.PrefetchScalarGridSpec(
        num_scalar_prefetch=1,
        grid=(nseq, nchunks),
        in_specs=[
            pl.BlockSpec((1, nstep, w), lambda b, c, pt: (b, 0, 0)),
            pl.BlockSpec((1, nstep, chunk), lambda b, c, pt: (b, 0, c)),
            pl.BlockSpec((1, nstep, LANES), lambda b, c, pt: (b, 0, tail_blk)),
            pl.BlockSpec((1, nstep, w), lambda b, c, pt: (b, 0, 0)),
            pl.BlockSpec((1, nstep, w), lambda b, c, pt: (b, 0, 0)),
            pl.BlockSpec(memory_space=pl.ANY),
            pl.BlockSpec(memory_space=pl.ANY),
        ],
        out_specs=pl.BlockSpec((1, nstep, w), lambda b, c, pt: (b, 0, 0)),
        scratch_shapes=[
            pltpu.VMEM((2, chunk, w), _F32), pltpu.VMEM((2, chunk, w), _F32),
            pltpu.SemaphoreType.DMA((2, 2)),
            pltpu.VMEM((rows, w), _BF16),
            pltpu.VMEM((rows, 1), _F32), pltpu.VMEM((rows, 1), _F32), pltpu.VMEM((rows, w), _F32),
        ],
    )
    return pl.pallas_call(
        functools.partial(_sample_attn_kernel, cpages=cpages, page=page, nstep=nstep),
        grid_spec=grid_spec,
        out_shape=jax.ShapeDtypeStruct((nseq, nstep, w), _BF16),
        compiler_params=pltpu.CompilerParams(
            dimension_semantics=("arbitrary", "arbitrary"), vmem_limit_bytes=VMEM_LIMIT_BYTES),
        name="sample_attn",
    )(page_table, q, bias, bias, k_new, v_new, cache_k, cache_v)


def _rope_tables(pos):
    half = HEAD_DIM // 2
    freq = ROPE_THETA ** (-jnp.arange(half, dtype=_F32) / half)
    ang = pos.astype(_F32)[:, None] * freq[None, :]
    cos, sin = jnp.cos(ang), jnp.sin(ang)
    reps = LANES // HEAD_DIM
    return (jnp.tile(jnp.concatenate([cos, cos], axis=1), (1, reps)),
            jnp.tile(jnp.concatenate([-sin, sin], axis=1), (1, reps)))


def _block_diag(w):
    nb, bw, _ = w.shape
    eye = jnp.eye(nb, dtype=w.dtype)
    return (w[:, :, None, :] * eye[:, None, :, None]).reshape(nb * bw, nb * bw)


def kernel(x_prompt, x_sample, cache_k, cache_v, cache_kidx, state_h, state_conv, page_table,
           g_pre_mix, w_in, conv_w, conv_b, w_a, b_a, w_i, b_i, lam, w_out, g_post_mix,
           g_pre_ffn, w_ff1, w_ff2, g_post_ffn):
    nb, seq, d = x_prompt.shape
    nseq, nstep, _ = x_sample.shape
    depth = g_pre_mix.shape[0]
    w = conv_b.shape[-1]
    heads = cache_k.shape[-2]
    page = cache_k.shape[2]
    past = page_table.shape[1] * page
    assert heads * HEAD_DIM == w and w % LANES == 0

    cos_p, sin_p = _rope_tables(jnp.arange(seq))
    cos_s, sin_s = _rope_tables(past + jnp.arange(nstep))
    cos_s, sin_s = jnp.repeat(cos_s, nseq, axis=0), jnp.repeat(sin_s, nseq, axis=0)

    xp = x_prompt
    xs = x_sample
    outs_p, outs_s = [], []
    for l in range(depth):
        row = lambda v: v[l].reshape(1, -1)
        main = 6 * w
        tail = w_in[l][:, main:]
        w_in_l = jnp.concatenate(
            [w_in[l][:, :main], tail, jnp.zeros((d, LANES - tail.shape[1]), w_in.dtype)],
            axis=1).astype(_BF16)
        w_gate = jnp.concatenate([_block_diag(w_a[l]), _block_diag(w_i[l])], axis=1).astype(_BF16)
        proj_w = (row(g_pre_mix), w_in_l, conv_w[l], row(conv_b), w_gate, row(b_a), row(b_i), row(lam))
        mlp_w = (w_out[l].astype(_BF16), row(g_post_mix), row(g_pre_ffn),
                 w_ff1[l].astype(_BF16), w_ff2[l].astype(_BF16), row(g_post_ffn))

        (k_p, v_p, kir_p, kw_p, q_p, qi_p, kt_p, kit_p, vb_p, lru_p, h_p, conv_p) = _prompt_proj(
            xp, cos_p, sin_p, *proj_w, tile=256)
        topk_p = min(TOPK_MAX, seq // 4)
        nqb = seq // QBLOCK
        group = 4
        att = []
        for jb in range(0, nqb, group):
            nblk = min(group, nqb - jb)
            att.append(_prompt_attn(qi_p, q_p, kw_p, kit_p, kt_p, vb_p,
                                    ncols=(jb + nblk) * QBLOCK, jbase=jb, nblocks=nblk, topk=topk_p))
        att_p = jnp.concatenate(att, axis=1)
        xp = _mlp(xp.reshape(nb * seq, d), lru_p.reshape(nb * seq, w), att_p.reshape(nb * seq, w),
                  *mlp_w, tile=256).reshape(nb, seq, d)
        outs_p.append((k_p.reshape(nb, seq, heads, HEAD_DIM), v_p.reshape(nb, seq, heads, HEAD_DIM),
                       kir_p[:, :, :IDX_DIM], h_p[:, 0, :], conv_p[:, SUBLANES - (CONV_W - 1):, :]))

        n_s = nseq * nstep
        to_steps = lambda v: jnp.swapaxes(v, 0, 1).reshape((-1,) + v.shape[2:])
        to_seqs = lambda v: jnp.swapaxes(v.reshape((-1, nseq) + v.shape[1:]), 0, 1)
        (k_s, v_s, kir_s, kw_s, q_s, qi_s, lru_s, h_s, conv_s) = _sample_proj(
            to_steps(xs), cos_s, sin_s, *proj_w, to_steps(state_conv[l]), state_h[l],
            nseq=nseq, nstep=nstep)
        k_s, v_s, kir_s, kw_s, q_s, qi_s, lru_s = map(to_seqs, (k_s, v_s, kir_s, kw_s, q_s, qi_s, lru_s))
        ki_s = kir_s[:, :, :IDX_DIM]
        ki_new = jnp.concatenate(
            [ki_s, jnp.zeros((nseq, LANES - nstep, IDX_DIM), _F32)], axis=1)
        qi_rows = qi_s.reshape(nseq, nstep * IDX_HEADS, IDX_DIM)
        wi_rows = kw_s[:, :, IDX_DIM:IDX_DIM + IDX_HEADS].reshape(nseq, nstep * IDX_HEADS, 1)
        sc = _sample_index(page_table, qi_rows, wi_rows, ki_new, cache_kidx[l], nstep=nstep)
        topk_s = min(TOPK_MAX, (past + nstep) // 4)
        bias = _sample_topk(sc.reshape(n_s, -1), past=past, nstep=nstep, topk=topk_s)
        att_s = _sample_attn(page_table, q_s, bias.reshape(nseq, nstep, -1), k_s, v_s,
                             cache_k[l].reshape(-1, page, w), cache_v[l].reshape(-1, page, w),
                             cpages=16)
        xs = _mlp(xs.reshape(n_s, d), lru_s.reshape(n_s, w), att_s.reshape(n_s, w),
                  *mlp_w, tile=n_s).reshape(nseq, nstep, d)
        outs_s.append((k_s.reshape(nseq, nstep, heads, HEAD_DIM),
                       v_s.reshape(nseq, nstep, heads, HEAD_DIM), ki_s, h_s,
                       jnp.swapaxes(conv_s.reshape(CONV_W - 1, nseq, w), 0, 1)))

    stack = lambda outs, i: jnp.stack([o[i] for o in outs])
    return (xp, xs) + tuple(stack(outs_p, i) for i in range(5)) + tuple(stack(outs_s, i) for i in range(5))
```

```python
import functools

import jax
import jax.numpy as jnp
from jax import lax
from jax.experimental import pallas as pl
from jax.experimental.pallas import tpu as pltpu

LRU_BLOCKS = 8
CONV_W = 4
RG_C = 8.0
HEAD_DIM = 64
IDX_HEADS = 8
IDX_DIM = 64
TOPK_MAX = 256
QBLOCK = 128
ROPE_THETA = 10000.0
EPS = 1e-6

LANES = 128
SUBLANES = 8
VMEM_LIMIT_BYTES = 48 * 1024 * 1024

_F32 = jnp.float32
_BF16 = jnp.bfloat16
_INT_MIN = -(2 ** 31)
_NT = (((1,), (1,)), ((), ()))


def _rmsnorm(x, g):
    return x * lax.rsqrt(jnp.mean(x * x, axis=-1, keepdims=True) + EPS) * g


def _swap_halves(x):
    w = x.shape[-1]
    lane = lax.broadcasted_iota(jnp.int32, x.shape, 1)
    fwd = pltpu.roll(x, w - HEAD_DIM // 2, 1)
    bwd = pltpu.roll(x, HEAD_DIM // 2, 1)
    return jnp.where((lane & (HEAD_DIM // 2)) == 0, fwd, bwd)


def _rope(x, cos, sin):
    return x * cos + _swap_halves(x) * sin


def _tile_lanes(t, n):
    return jnp.concatenate([t] * n, axis=1) if n > 1 else t


def _project(x, cos, sin, g_ref, win_ref, width):
    hn = _rmsnorm(x, g_ref[...]).astype(_BF16)
    z = jnp.dot(hn, win_ref[...], preferred_element_type=_F32)
    w = width
    n = w // LANES
    cos_w, sin_w = _tile_lanes(cos, n), _tile_lanes(sin, n)
    out = {
        "xl": z[:, 0:w],
        "gate": z[:, w:2 * w],
        "q": _rope(z[:, 2 * w:3 * w], cos_w, sin_w),
        "k": _rope(z[:, 3 * w:4 * w], cos_w, sin_w),
        "v": z[:, 4 * w:5 * w],
        "qi": _rope(z[:, 5 * w:6 * w], cos_w, sin_w),
        "kw": z[:, 6 * w:6 * w + LANES],
    }
    out["kir"] = _rope(out["kw"], cos, sin)
    return out


def _expm1(x):
    u = jnp.exp(x)
    d = u - 1.0
    return jnp.where(u == 1.0, x, jnp.where(d == -1.0, -1.0, d * x / jnp.log(u)))


def _lru_gates(xc, wg_ref, ba, bi, lam):
    w = xc.shape[1]
    g = jnp.dot(xc.astype(_BF16), wg_ref[...], preferred_element_type=_F32)
    r = jax.nn.sigmoid(g[:, :w] + ba)
    i = jax.nn.sigmoid(g[:, w:] + bi)
    log_a = (-RG_C * jax.nn.softplus(-lam)) * r
    a = jnp.exp(log_a)
    u = jnp.sqrt(-_expm1(2.0 * log_a)) * (i * xc)
    return a, u


def _scan_rows(a, u):
    tm, w = a.shape
    row = lax.broadcasted_iota(jnp.int32, a.shape, 0)
    s = 1
    while s < tm:
        if s < SUBLANES:
            keep = row >= s
            a_sh = jnp.where(keep, pltpu.roll(a, s, 0), 1.0)
            u_sh = jnp.where(keep, pltpu.roll(u, s, 0), 0.0)
        else:
            a_sh = jnp.concatenate([jnp.ones((s, w), _F32), a[:tm - s]], axis=0)
            u_sh = jnp.concatenate([jnp.zeros((s, w), _F32), u[:tm - s]], axis=0)
        u = a * u_sh + u
        a = a * a_sh
        s *= 2
    return a, u


def _prompt_proj_kernel(x_ref, cos_ref, sin_ref, g_ref, win_ref, convw_ref, convb_ref, wg_ref,
                        ba_ref, bi_ref, lam_ref,
                        ktf_ref, vtf_ref, kir_ref, kw_ref, q_ref, qi_ref, kt_ref, kit_ref, vb_ref,
                        lru_ref, h_ref, conv_ref, hcar_ref, ccar_ref, *, width):
    t = pl.program_id(1)

    @pl.when(t == 0)
    def _():
        hcar_ref[...] = jnp.zeros_like(hcar_ref)
        ccar_ref[...] = jnp.zeros_like(ccar_ref)

    x = x_ref[0]
    tm = x.shape[0]
    p = _project(x, cos_ref[...], sin_ref[...], g_ref, win_ref, width)
    scale = HEAD_DIM ** -0.5
    kt = p["k"].T
    ktf_ref[0] = kt
    vtf_ref[0] = p["v"].T
    kir_ref[0] = p["kir"]
    kw_ref[0] = p["kw"]
    q_ref[0] = (p["q"] * scale).astype(_BF16)
    qi_ref[0] = (p["qi"] * (IDX_DIM ** -0.5)).astype(_BF16)
    kt_ref[0] = kt.astype(_BF16)
    ki = p["kir"][:, :IDX_DIM]
    kit_ref[0] = jnp.concatenate([ki, ki], axis=1).T.astype(_BF16)
    vb_ref[0] = p["v"].astype(_BF16)

    xl = p["xl"]
    ext = jnp.concatenate([ccar_ref[...], xl], axis=0)
    xc = convb_ref[...]
    for j in range(CONV_W):
        lo = SUBLANES - (CONV_W - 1) + j
        xc = xc + ext[lo:lo + tm] * convw_ref[j:j + 1, :]
    ccar_ref[...] = xl[tm - SUBLANES:]
    conv_ref[0] = xl[tm - SUBLANES:]

    a, u = _lru_gates(xc, wg_ref, ba_ref[...], bi_ref[...], lam_ref[...])
    a_cum, u_cum = _scan_rows(a, u)
    hs = a_cum * hcar_ref[0:1, :] + u_cum
    h_last = jnp.broadcast_to(hs[tm - 1:tm, :], (SUBLANES, width))
    hcar_ref[...] = h_last
    h_ref[0] = h_last
    lru_ref[0] = (hs * jax.nn.gelu(p["gate"])).astype(_BF16)


def _prompt_proj(x, cos, sin, g_pre, w_in, conv_w, conv_b, w_gate, b_a, b_i, lam, *, tile):
    b, t, d = x.shape
    w = conv_b.shape[-1]
    nt = t // tile
    const = lambda shape: pl.BlockSpec(shape, lambda bi, ti: (0,) * len(shape))
    tok = lambda last: pl.BlockSpec((1, tile, last), lambda bi, ti: (bi, ti, 0))
    state = pl.BlockSpec((1, SUBLANES, w), lambda bi, ti: (bi, 0, 0))
    out_shape = (
        jax.ShapeDtypeStruct((b, w, t), _F32),
        jax.ShapeDtypeStruct((b, w, t), _F32),
        jax.ShapeDtypeStruct((b, t, LANES), _F32),
        jax.ShapeDtypeStruct((b, t, LANES), _F32),
        jax.ShapeDtypeStruct((b, t, w), _BF16),
        jax.ShapeDtypeStruct((b, t, w), _BF16),
        jax.ShapeDtypeStruct((b, w, t), _BF16),
        jax.ShapeDtypeStruct((b, LANES, t), _BF16),
        jax.ShapeDtypeStruct((b, t, w), _BF16),
        jax.ShapeDtypeStruct((b, t, w), _BF16),
        jax.ShapeDtypeStruct((b, SUBLANES, w), _F32),
        jax.ShapeDtypeStruct((b, SUBLANES, w), _F32),
    )
    out_specs = (
        pl.BlockSpec((1, w, tile), lambda bi, ti: (bi, 0, ti)),
        pl.BlockSpec((1, w, tile), lambda bi, ti: (bi, 0, ti)),
        tok(LANES), tok(LANES), tok(w), tok(w),
        pl.BlockSpec((1, w, tile), lambda bi, ti: (bi, 0, ti)),
        pl.BlockSpec((1, LANES, tile), lambda bi, ti: (bi, 0, ti)),
        tok(w), tok(w), state, state,
    )
    in_specs = [
        tok(d),
        pl.BlockSpec((tile, LANES), lambda bi, ti: (ti, 0)),
        pl.BlockSpec((tile, LANES), lambda bi, ti: (ti, 0)),
        const((1, d)), const(w_in.shape), const(conv_w.shape), const((1, w)), const(w_gate.shape),
        const((1, w)), const((1, w)), const((1, w)),
    ]
    return pl.pallas_call(
        functools.partial(_prompt_proj_kernel, width=w),
        grid=(b, nt),
        in_specs=in_specs,
        out_specs=out_specs,
        out_shape=out_shape,
        scratch_shapes=[pltpu.VMEM((SUBLANES, w), _F32), pltpu.VMEM((SUBLANES, w), _F32)],
        compiler_params=pltpu.CompilerParams(
            dimension_semantics=("arbitrary", "arbitrary"), vmem_limit_bytes=VMEM_LIMIT_BYTES),
        name="prompt_proj",
    )(x, cos, sin, g_pre, w_in, conv_w, conv_b, w_gate, b_a, b_i, lam)


def _sortable_key(score, admissible):
    score = jnp.where(score == 0.0, 0.0, score)
    bits = pltpu.bitcast(score, jnp.int32)
    key = bits ^ ((bits >> 31) & 0x7FFFFFFF)
    return jnp.where(admissible, key, _INT_MIN)


def _topk_bias(key_ref, bias_ref, jcut_ref, *, ncols, topk):
    rows = key_ref.shape[0]
    nchunk = ncols // LANES
    lane = lax.broadcasted_iota(jnp.int32, (rows, LANES), 1)
    kf = float(topk)

    def count(pred):
        acc = jnp.zeros((rows, LANES), _F32)
        for c in range(nchunk):
            acc = acc + pred(key_ref[:, c * LANES:(c + 1) * LANES], c)
        return jnp.broadcast_to(jnp.sum(acc, axis=1, keepdims=True), (rows, LANES))

    def thr_step(i, thr):
        cand = thr + jnp.left_shift(jnp.int32(1), 31 - i)
        cnt = count(lambda k, c: jnp.where(k >= cand, 1.0, 0.0))
        return jnp.where(cnt >= kf, cand, thr)

    thr = lax.fori_loop(0, 32, thr_step, jnp.full((rows, LANES), _INT_MIN, jnp.int32))

    need = kf - count(lambda k, c: jnp.where(k > thr, 1.0, 0.0))
    n_eq = count(lambda k, c: jnp.where(k == thr, 1.0, 0.0))
    jcut_ref[...] = jnp.full((rows, LANES), ncols, jnp.int32)

    @pl.when(jnp.max(n_eq - need) > 0.0)
    def _():
        nbits = max(1, (ncols - 1).bit_length())

        def cut_step(i, cut):
            cand = cut + jnp.left_shift(jnp.int32(1), nbits - 1 - i)
            below = count(lambda k, c: jnp.where(
                k == thr, jnp.where(c * LANES + lane < cand, 1.0, 0.0), 0.0))
            return jnp.where(below < need, cand, cut)

        cut = lax.fori_loop(0, nbits, cut_step, jnp.zeros((rows, LANES), jnp.int32))
        jcut_ref[...] = cut + 1

    jcut = jnp.where(thr == _INT_MIN, 0, jcut_ref[...])
    for c in range(nchunk):
        k = key_ref[:, c * LANES:(c + 1) * LANES]
        tie = jnp.where(c * LANES + lane < jcut, 0.0, -jnp.inf)
        bias_ref[:, c * LANES:(c + 1) * LANES] = jnp.where(
            k > thr, 0.0, jnp.where(k == thr, tie, -jnp.inf))


def _prompt_attn_kernel(qi_ref, q_ref, kw_ref, kit_ref, kt_ref, v_ref, o_ref,
                        key_ref, bias_ref, jcut_ref, wib_ref, *, ncols, jbase, topk):
    j = jbase + pl.program_id(1)
    qb = q_ref.shape[1]
    width = q_ref.shape[2]
    heads = width // HEAD_DIM
    lane_w = lax.broadcasted_iota(jnp.int32, (qb, width), 1)
    even = (lane_w & HEAD_DIM) == 0

    kw = kw_ref[0]
    for h in range(IDX_HEADS):
        col = IDX_DIM + h
        wib_ref[h] = jnp.broadcast_to(kw[:, col:col + 1], (qb, LANES))

    qi = qi_ref[0]
    zero = jnp.zeros_like(qi)
    qi_par = (jnp.where(even, qi, zero), jnp.where(even, zero, qi))
    cw = 2 * LANES
    qpos = j * qb + lax.broadcasted_iota(jnp.int32, (qb, cw), 0)
    for c in range(ncols // cw):
        kic = kit_ref[0, :, c * cw:(c + 1) * cw]
        acc = jnp.zeros((qb, cw), _F32)
        for h in range(IDX_HEADS):
            lhs = qi_par[h % 2][:, (h // 2) * LANES:(h // 2 + 1) * LANES]
            d = jnp.dot(lhs, kic, preferred_element_type=_F32)
            acc = acc + jnp.maximum(d, 0.0) * _tile_lanes(wib_ref[h], cw // LANES)
        score = acc * (IDX_HEADS ** -0.5)
        col = c * cw + lax.broadcasted_iota(jnp.int32, (qb, cw), 1)
        key_ref[:, c * cw:(c + 1) * cw] = _sortable_key(score, col <= qpos)

    _topk_bias(key_ref, bias_ref, jcut_ref, ncols=ncols, topk=topk)

    q = q_ref[0]
    zero = jnp.zeros_like(q)
    q_par = (jnp.where(even, q, zero), jnp.where(even, zero, q))
    lane = lax.broadcasted_iota(jnp.int32, (qb, LANES), 1)
    for g in range(heads // 2):
        sl = slice(g * LANES, (g + 1) * LANES)
        kt = kt_ref[0, sl, :]
        vv = v_ref[0, :, sl]
        outs = []
        for par in range(2):
            s = jnp.dot(q_par[par][:, sl], kt, preferred_element_type=_F32) + bias_ref[...]
            m = jnp.max(s, axis=1, keepdims=True)
            p = jnp.exp(s - m)
            l = jnp.sum(p, axis=1, keepdims=True)
            o = jnp.dot(p.astype(_BF16), vv, preferred_element_type=_F32)
            outs.append(o * (1.0 / l))
        o_ref[0, :, sl] = jnp.where(lane < HEAD_DIM, outs[0], outs[1]).astype(o_ref.dtype)


def _prompt_attn(qi, q, kw, kit, kt, vb, *, ncols, jbase, nblocks, topk):
    b, t, w = q.shape
    qblk = lambda last: pl.BlockSpec((1, QBLOCK, last), lambda bi, ji: (bi, jbase + ji, 0))
    return pl.pallas_call(
        functools.partial(_prompt_attn_kernel, ncols=ncols, jbase=jbase, topk=topk),
        grid=(b, nblocks),
        in_specs=[
            qblk(w), qblk(w), qblk(LANES),
            pl.BlockSpec((1, LANES, ncols), lambda bi, ji: (bi, 0, 0)),
            pl.BlockSpec((1, w, ncols), lambda bi, ji: (bi, 0, 0)),
            pl.BlockSpec((1, ncols, w), lambda bi, ji: (bi, 0, 0)),
        ],
        out_specs=pl.BlockSpec((1, QBLOCK, w), lambda bi, ji: (bi, ji, 0)),
        out_shape=jax.ShapeDtypeStruct((b, nblocks * QBLOCK, w), _BF16),
        scratch_shapes=[
            pltpu.VMEM((QBLOCK, ncols), jnp.int32),
            pltpu.VMEM((QBLOCK, ncols), _F32),
            pltpu.VMEM((QBLOCK, LANES), jnp.int32),
            pltpu.VMEM((IDX_HEADS, QBLOCK, LANES), _F32),
        ],
        compiler_params=pltpu.CompilerParams(
            dimension_semantics=("arbitrary", "arbitrary"), vmem_limit_bytes=VMEM_LIMIT_BYTES),
        name=f"prompt_attn_{ncols}",
    )(qi, q, kw, kit, kt, vb)


def _mlp_kernel(x_ref, lru_ref, att_ref, wout_ref, gpm_ref, gpf_ref, w1_ref, w2_ref, gpo_ref, y_ref):
    w = lru_ref.shape[1]
    mix = (jnp.dot(lru_ref[...], wout_ref[0:w, :], preferred_element_type=_F32)
           + jnp.dot(att_ref[...], wout_ref[w:, :], preferred_element_type=_F32))
    x1 = x_ref[...] + _rmsnorm(mix, gpm_ref[...])
    hn = _rmsnorm(x1, gpf_ref[...]).astype(_BF16)
    f = jnp.dot(hn, w1_ref[...], preferred_element_type=_F32)
    f = jnp.square(jnp.maximum(f, 0.0)).astype(_BF16)
    f2 = jnp.dot(f, w2_ref[...], preferred_element_type=_F32)
    y_ref[...] = x1 + _rmsnorm(f2, gpo_ref[...])


def _mlp(x, lru, att, w_out, g_post_mix, g_pre_ffn, w_ff1, w_ff2, g_post_ffn, *, tile):
    n, d = x.shape
    w = lru.shape[1]
    const = lambda shape: pl.BlockSpec(shape, lambda i: (0,) * len(shape),
                                       pipeline_mode=pl.Buffered(1))
    tok = lambda last: pl.BlockSpec((tile, last), lambda i: (i, 0))
    return pl.pallas_call(
        _mlp_kernel,
        grid=(n // tile,),
        in_specs=[tok(d), tok(w), tok(w), const(w_out.shape), const((1, d)), const((1, d)),
                  const(w_ff1.shape), const(w_ff2.shape), const((1, d))],
        out_specs=tok(d),
        out_shape=jax.ShapeDtypeStruct((n, d), _F32),
        compiler_params=pltpu.CompilerParams(
            dimension_semantics=("arbitrary",), vmem_limit_bytes=VMEM_LIMIT_BYTES),
        name="mlp",
    )(x, lru, att, w_out, g_post_mix, g_pre_ffn, w_ff1, w_ff2, g_post_ffn)


def _sample_proj_kernel(x_ref, cos_ref, sin_ref, g_ref, win_ref, convw_ref, convb_ref, wg_ref,
                        ba_ref, bi_ref, lam_ref, cst_ref, h0_ref,
                        k_ref, v_ref, kir_ref, kw_ref, q_ref, qi_ref, lru_ref, h_ref, cnew_ref,
                        *, width, nseq, nstep):
    p = _project(x_ref[...], cos_ref[...], sin_ref[...], g_ref, win_ref, width)
    k_ref[...] = p["k"]
    v_ref[...] = p["v"]
    kir_ref[...] = p["kir"]
    kw_ref[...] = p["kw"]
    q_ref[...] = (p["q"] * (HEAD_DIM ** -0.5)).astype(_BF16)
    qi_ref[...] = (p["qi"] * (IDX_DIM ** -0.5)).astype(_BF16)

    xl = p["xl"]
    n = nseq * nstep
    ext = jnp.concatenate([cst_ref[...], xl], axis=0)
    xc = convb_ref[...]
    for j in range(CONV_W):
        xc = xc + ext[j * nseq:j * nseq + n] * convw_ref[j:j + 1, :]
    cnew_ref[...] = ext[n:]

    a, u = _lru_gates(xc, wg_ref, ba_ref[...], bi_ref[...], lam_ref[...])
    h = h0_ref[...]
    hs = []
    for s in range(nstep):
        h = a[s * nseq:(s + 1) * nseq] * h + u[s * nseq:(s + 1) * nseq]
        hs.append(h)
    h_ref[...] = h
    lru_ref[...] = (jnp.concatenate(hs, axis=0) * jax.nn.gelu(p["gate"])).astype(_BF16)


def _sample_proj(x, cos, sin, g_pre, w_in, conv_w, conv_b, w_gate, b_a, b_i, lam, cst, h0,
                 *, nseq, nstep):
    n, d = x.shape
    w = conv_b.shape[-1]
    out_shape = (
        jax.ShapeDtypeStruct((n, w), _F32), jax.ShapeDtypeStruct((n, w), _F32),
        jax.ShapeDtypeStruct((n, LANES), _F32), jax.ShapeDtypeStruct((n, LANES), _F32),
        jax.ShapeDtypeStruct((n, w), _BF16), jax.ShapeDtypeStruct((n, w), _BF16),
        jax.ShapeDtypeStruct((n, w), _BF16),
        jax.ShapeDtypeStruct((nseq, w), _F32),
        jax.ShapeDtypeStruct(((CONV_W - 1) * nseq, w), _F32),
    )
    return pl.pallas_call(
        functools.partial(_sample_proj_kernel, width=w, nseq=nseq, nstep=nstep),
        out_shape=out_shape,
        compiler_params=pltpu.CompilerParams(vmem_limit_bytes=VMEM_LIMIT_BYTES),
        name="sample_proj",
    )(x, cos, sin, g_pre, w_in, conv_w, conv_b, w_gate, b_a, b_i, lam, cst, h0)


def _sample_index_kernel(pt_ref, qi_ref, wi_ref, kin_ref, cache_ref, sc_ref, buf_ref, sem_ref,
                         *, npages, page, nstep):
    b = pl.program_id(0)
    nb = pl.num_programs(0)
    slot = b % 2

    def page_copy(seq, p, sl):
        return pltpu.make_async_copy(cache_ref.at[pt_ref[seq, p]],
                                     buf_ref.at[sl, :, pl.ds(p * page, page)], sem_ref.at[sl])

    @pl.when(b == 0)
    def _():
        for p in range(npages):
            page_copy(0, p, 0).start()

    @pl.when(b + 1 < nb)
    def _():
        for p in range(npages):
            page_copy(b + 1, p, 1 - slot).start()

    for p in range(npages):
        page_copy(b, p, slot).wait()

    qi = qi_ref[0]
    wi = wi_ref[0]

    def scores(keys_t):
        n = keys_t.shape[1]
        d = jnp.dot(qi, keys_t.astype(_BF16), preferred_element_type=_F32)
        wd = jnp.maximum(d, 0.0) * wi
        return jnp.sum(wd.reshape(nstep, IDX_HEADS, n), axis=1) * (IDX_HEADS ** -0.5)

    past = npages * page
    chunk = 16 * page
    for c in range(past // chunk):
        sc_ref[0, :, c * chunk:(c + 1) * chunk] = scores(buf_ref[slot, :, c * chunk:(c + 1) * chunk])
    sc_ref[0, :, past:] = scores(kin_ref[0])


def _sample_index(page_table, qi_rows, wi_rows, ki_new, cache_kidx, *, nstep):
    nseq, npages = page_table.shape
    page = cache_kidx.shape[2]
    past = npages * page
    ncols = past + LANES
    grid_spec = pltpu.PrefetchScalarGridSpec(
        num_scalar_prefetch=1,
        grid=(nseq,),
        in_specs=[
            pl.BlockSpec((1,) + qi_rows.shape[1:], lambda b, pt: (b, 0, 0)),
            pl.BlockSpec((1,) + wi_rows.shape[1:], lambda b, pt: (b, 0, 0)),
            pl.BlockSpec((1,) + ki_new.shape[1:], lambda b, pt: (b, 0, 0)),
            pl.BlockSpec(memory_space=pl.ANY),
        ],
        out_specs=pl.BlockSpec((1, nstep, ncols), lambda b, pt: (b, 0, 0)),
        scratch_shapes=[pltpu.VMEM((2, IDX_DIM, past), _F32), pltpu.SemaphoreType.DMA((2,))],
    )
    return pl.pallas_call(
        functools.partial(_sample_index_kernel, npages=npages, page=page, nstep=nstep),
        grid_spec=grid_spec,
        out_shape=jax.ShapeDtypeStruct((nseq, nstep, ncols), _F32),
        compiler_params=pltpu.CompilerParams(
            dimension_semantics=("arbitrary",), vmem_limit_bytes=VMEM_LIMIT_BYTES),
        name="sample_index",
    )(page_table, qi_rows, wi_rows, ki_new, cache_kidx)


def _sample_topk_kernel(sc_ref, bias_ref, key_ref, jcut_ref, *, past, nstep, topk):
    rows, ncols = sc_ref.shape
    cw = 4 * LANES
    step = lax.broadcasted_iota(jnp.int32, (rows, cw), 0) % nstep
    for c in range(ncols // cw + (1 if ncols % cw else 0)):
        lo = c * cw
        wdt = min(cw, ncols - lo)
        col = lo + lax.broadcasted_iota(jnp.int32, (rows, wdt), 1)
        key_ref[:, lo:lo + wdt] = _sortable_key(sc_ref[:, lo:lo + wdt], col <= past + step[:, :wdt])
    _topk_bias(key_ref, bias_ref, jcut_ref, ncols=ncols, topk=topk)


def _sample_topk(sc, *, past, nstep, topk):
    rows, ncols = sc.shape
    return pl.pallas_call(
        functools.partial(_sample_topk_kernel, past=past, nstep=nstep, topk=topk),
        out_shape=jax.ShapeDtypeStruct((rows, ncols), _F32),
        scratch_shapes=[pltpu.VMEM((rows, ncols), jnp.int32), pltpu.VMEM((rows, LANES), jnp.int32)],
        compiler_params=pltpu.CompilerParams(vmem_limit_bytes=VMEM_LIMIT_BYTES),
        name="sample_topk",
    )(sc)


def _sample_attn_kernel(pt_ref, q_ref, bias_ref, btail_ref, kn_ref, vn_ref, ck_ref, cv_ref, o_ref,
                        kbuf_ref, vbuf_ref, sem_ref, qbd_ref, m_ref, l_ref, acc_ref,
                        *, cpages, page, nstep):
    b = pl.program_id(0)
    c = pl.program_id(1)
    nchunks = pl.num_programs(1)
    step = b * nchunks + c
    total = pl.num_programs(0) * nchunks
    slot = step % 2
    width = q_ref.shape[2]
    heads = width // HEAD_DIM
    rows = nstep * heads

    def copies(seq, chunk, p, sl):
        phys = pt_ref[seq, chunk * cpages + p]
        dst = pl.ds(p * page, page)
        return (pltpu.make_async_copy(ck_ref.at[phys], kbuf_ref.at[sl, :, dst], sem_ref.at[0, sl]),
                pltpu.make_async_copy(cv_ref.at[phys], vbuf_ref.at[sl, :, dst], sem_ref.at[1, sl]))

    def fetch(seq, chunk, sl):
        for p in range(cpages):
            for cp in copies(seq, chunk, p, sl):
                cp.start()

    @pl.when(step == 0)
    def _():
        fetch(0, 0, 0)

    @pl.when(step + 1 < total)
    def _():
        nxt = step + 1
        fetch(nxt // nchunks, nxt % nchunks, 1 - slot)

    for p in range(cpages):
        for cp in copies(b, c, p, slot):
            cp.wait()

    row = lax.broadcasted_iota(jnp.int32, (rows, width), 0)
    lane = lax.broadcasted_iota(jnp.int32, (rows, width), 1)
    own_head = (row % heads) == (lane // HEAD_DIM)

    def expand(x):
        return jnp.broadcast_to(x[:, None, :], (nstep, heads, x.shape[1])).reshape(rows, x.shape[1])

    @pl.when(c == 0)
    def _():
        q = expand(q_ref[0].astype(_F32))
        qbd_ref[...] = jnp.where(own_head, q, 0.0).astype(_BF16)
        m_ref[...] = jnp.full(m_ref.shape, -1e30, _F32)
        l_ref[...] = jnp.zeros_like(l_ref)
        acc_ref[...] = jnp.zeros_like(acc_ref)

    s = jnp.dot(qbd_ref[...], kbuf_ref[slot].astype(_BF16), preferred_element_type=_F32)
    s = s + expand(bias_ref[0])
    m_old = m_ref[...]
    m_new = jnp.maximum(m_old, jnp.max(s, axis=1, keepdims=True))
    alpha = jnp.exp(m_old - m_new)
    pr = jnp.exp(s - m_new)
    l_ref[...] = alpha * l_ref[...] + jnp.sum(pr, axis=1, keepdims=True)
    acc_ref[...] = alpha * acc_ref[...] + lax.dot_general(
        pr.astype(_BF16), vbuf_ref[slot].astype(_BF16), _NT, preferred_element_type=_F32)
    m_ref[...] = m_new

    @pl.when(c == nchunks - 1)
    def _():
        qf = qbd_ref[...].astype(_F32)
        kn = kn_ref[0].astype(_BF16).astype(_F32)
        vn = vn_ref[0].astype(_BF16).astype(_F32)
        bt = expand(btail_ref[0])
        sn = [jnp.sum(qf * kn[t:t + 1, :], axis=1, keepdims=True) + bt[:, t:t + 1]
              for t in range(nstep)]
        m_old = m_ref[...]
        m_new = m_old
        for t in range(nstep):
            m_new = jnp.maximum(m_new, sn[t])
        alpha = jnp.exp(m_old - m_new)
        l = alpha * l_ref[...]
        acc = alpha * acc_ref[...]
        for t in range(nstep):
            pt_ = jnp.exp(sn[t] - m_new)
            l = l + pt_
            acc = acc + pt_.astype(_BF16).astype(_F32) * vn[t:t + 1, :]
        o = jnp.where(own_head, acc * (1.0 / l), 0.0)
        o_ref[0] = jnp.sum(o.reshape(nstep, heads, width), axis=1).astype(o_ref.dtype)


def _sample_attn(page_table, q, bias, k_new, v_new, cache_k, cache_v, *, cpages):
    nseq, nstep, w = q.shape
    npages = page_table.shape[1]
    page = cache_k.shape[2]
    chunk = cpages * page
    nchunks = npages // cpages
    rows = nstep * (w // HEAD_DIM)
    tail_blk = (npages * page) // LANES
    grid_spec = pltpu.PrefetchScalarGridSpec(
        num_scalar_prefetch=1,
        grid=(nseq, nchunks),
        in_specs=[
            pl.BlockSpec((1, nstep, w), lambda b, c, pt: (b, 0, 0)),
            pl.BlockSpec((1, nstep, chunk), lambda b, c, pt: (b, 0, c)),
            pl.BlockSpec((1, nstep, LANES), lambda b, c, pt: (b, 0, tail_blk)),
            pl.BlockSpec((1, nstep, w), lambda b, c, pt: (b, 0, 0)),
            pl.BlockSpec((1, nstep, w), lambda b, c, pt: (b, 0, 0)),
            pl.BlockSpec(memory_space=pl.ANY),
            pl.BlockSpec(memory_space=pl.ANY),
        ],
        out_specs=pl.BlockSpec((1, nstep, w), lambda b, c, pt: (b, 0, 0)),
        scratch_shapes=[
            pltpu.VMEM((2, w, chunk), _F32), pltpu.VMEM((2, w, chunk), _F32),
            pltpu.SemaphoreType.DMA((2, 2)),
            pltpu.VMEM((rows, w), _BF16),
            pltpu.VMEM((rows, 1), _F32), pltpu.VMEM((rows, 1), _F32), pltpu.VMEM((rows, w), _F32),
        ],
    )
    return pl.pallas_call(
        functools.partial(_sample_attn_kernel, cpages=cpages, page=page, nstep=nstep),
        grid_spec=grid_spec,
        out_shape=jax.ShapeDtypeStruct((nseq, nstep, w), _BF16),
        compiler_params=pltpu.CompilerParams(
            dimension_semantics=("arbitrary", "arbitrary"), vmem_limit_bytes=VMEM_LIMIT_BYTES),
        name="sample_attn",
    )(page_table, q, bias, bias, k_new, v_new, cache_k, cache_v)


def _rope_tables(pos):
    half = HEAD_DIM // 2
    freq = ROPE_THETA ** (-jnp.arange(half, dtype=_F32) / half)
    ang = pos.astype(_F32)[:, None] * freq[None, :]
    cos, sin = jnp.cos(ang), jnp.sin(ang)
    reps = LANES // HEAD_DIM
    return (jnp.tile(jnp.concatenate([cos, cos], axis=1), (1, reps)),
            jnp.tile(jnp.concatenate([-sin, sin], axis=1), (1, reps)))


def _block_diag(w):
    nb, bw, _ = w.shape
    eye = jnp.eye(nb, dtype=w.dtype)
    return (w[:, :, None, :] * eye[:, None, :, None]).reshape(nb * bw, nb * bw)


def kernel(x_prompt, x_sample, cache_k, cache_v, cache_kidx, state_h, state_conv, page_table,
           g_pre_mix, w_in, conv_w, conv_b, w_a, b_a, w_i, b_i, lam, w_out, g_post_mix,
           g_pre_ffn, w_ff1, w_ff2, g_post_ffn):
    nb, seq, d = x_prompt.shape
    nseq, nstep, _ = x_sample.shape
    depth = g_pre_mix.shape[0]
    w = conv_b.shape[-1]
    heads = cache_k.shape[-2]
    page = cache_k.shape[2]
    past = page_table.shape[1] * page
    assert heads * HEAD_DIM == w and w % LANES == 0

    cos_p, sin_p = _rope_tables(jnp.arange(seq))
    cos_s, sin_s = _rope_tables(past + jnp.arange(nstep))
    cos_s, sin_s = jnp.repeat(cos_s, nseq, axis=0), jnp.repeat(sin_s, nseq, axis=0)

    xp = x_prompt
    xs = x_sample
    outs_p, outs_s = [], []
    for l in range(depth):
        row = lambda v: v[l].reshape(1, -1)
        main = 6 * w
        tail = w_in[l][:, main:]
        w_in_l = jnp.concatenate(
            [w_in[l][:, :main], tail, jnp.zeros((d, LANES - tail.shape[1]), w_in.dtype)],
            axis=1).astype(_BF16)
        w_gate = jnp.concatenate([_block_diag(w_a[l]), _block_diag(w_i[l])], axis=1).astype(_BF16)
        proj_w = (row(g_pre_mix), w_in_l, conv_w[l], row(conv_b), w_gate, row(b_a), row(b_i), row(lam))
        mlp_w = (w_out[l].astype(_BF16), row(g_post_mix), row(g_pre_ffn),
                 w_ff1[l].astype(_BF16), w_ff2[l].astype(_BF16), row(g_post_ffn))

        (ktf_p, vtf_p, kir_p, kw_p, q_p, qi_p, kt_p, kit_p, vb_p, lru_p, h_p, conv_p) = _prompt_proj(
            xp, cos_p, sin_p, *proj_w, tile=256)
        topk_p = min(TOPK_MAX, seq // 4)
        nqb = seq // QBLOCK
        group = 2
        att = []
        for jb in range(0, nqb, group):
            nblk = min(group, nqb - jb)
            att.append(_prompt_attn(qi_p, q_p, kw_p, kit_p, kt_p, vb_p,
                                    ncols=(jb + nblk) * QBLOCK, jbase=jb, nblocks=nblk, topk=topk_p))
        att_p = jnp.concatenate(att, axis=1)
        xp = _mlp(xp.reshape(nb * seq, d), lru_p.reshape(nb * seq, w), att_p.reshape(nb * seq, w),
                  *mlp_w, tile=256).reshape(nb, seq, d)
        from_t = lambda v: jnp.transpose(v.reshape(nb, heads, HEAD_DIM, seq), (0, 3, 1, 2))
        outs_p.append((from_t(ktf_p), from_t(vtf_p),
                       kir_p[:, :, :IDX_DIM], h_p[:, 0, :], conv_p[:, SUBLANES - (CONV_W - 1):, :]))

        n_s = nseq * nstep
        to_steps = lambda v: jnp.swapaxes(v, 0, 1).reshape((-1,) + v.shape[2:])
        to_seqs = lambda v: jnp.swapaxes(v.reshape((-1, nseq) + v.shape[1:]), 0, 1)
        (k_s, v_s, kir_s, kw_s, q_s, qi_s, lru_s, h_s, conv_s) = _sample_proj(
            to_steps(xs), cos_s, sin_s, *proj_w, to_steps(state_conv[l]), state_h[l],
            nseq=nseq, nstep=nstep)
        k_s, v_s, kir_s, kw_s, q_s, qi_s, lru_s = map(to_seqs, (k_s, v_s, kir_s, kw_s, q_s, qi_s, lru_s))
        ki_s = kir_s[:, :, :IDX_DIM]
        ki_new = jnp.swapaxes(jnp.concatenate(
            [ki_s, jnp.zeros((nseq, LANES - nstep, IDX_DIM), _F32)], axis=1), 1, 2)
        qi_rows = qi_s.reshape(nseq, nstep * IDX_HEADS, IDX_DIM)
        wi_rows = kw_s[:, :, IDX_DIM:IDX_DIM + IDX_HEADS].reshape(nseq, nstep * IDX_HEADS, 1)
        kidx_t = jnp.swapaxes(cache_kidx[l], 1, 2)
        k_t = jnp.transpose(cache_k[l], (0, 2, 3, 1)).reshape(-1, w, page)
        v_t = jnp.transpose(cache_v[l], (0, 2, 3, 1)).reshape(-1, w, page)
        sc = _sample_index(page_table, qi_rows, wi_rows, ki_new, kidx_t, nstep=nstep)
        topk_s = min(TOPK_MAX, (past + nstep) // 4)
        bias = _sample_topk(sc.reshape(n_s, -1), past=past, nstep=nstep, topk=topk_s)
        att_s = _sample_attn(page_table, q_s, bias.reshape(nseq, nstep, -1), k_s, v_s, k_t, v_t,
                             cpages=16)
        xs = _mlp(xs.reshape(n_s, d), lru_s.reshape(n_s, w), att_s.reshape(n_s, w),
                  *mlp_w, tile=n_s).reshape(nseq, nstep, d)
        outs_s.append((k_s.reshape(nseq, nstep, heads, HEAD_DIM),
                       v_s.reshape(nseq, nstep, heads, HEAD_DIM), ki_s, h_s,
                       jnp.swapaxes(conv_s.reshape(CONV_W - 1, nseq, w), 0, 1)))

    stack = lambda outs, i: jnp.stack([o[i] for o in outs])
    return (xp, xs) + tuple(stack(outs_p, i) for i in range(5)) + tuple(stack(outs_s, i) for i in range(5))
```

```python
import functools

import jax
import jax.numpy as jnp
from jax import lax
from jax.experimental import pallas as pl
from jax.experimental.pallas import tpu as pltpu

LRU_BLOCKS = 8
CONV_W = 4
RG_C = 8.0
HEAD_DIM = 64
IDX_HEADS = 8
IDX_DIM = 64
TOPK_MAX = 256
QBLOCK = 128
ROPE_THETA = 10000.0
EPS = 1e-6

LANES = 128
SUBLANES = 8
VMEM_LIMIT_BYTES = 48 * 1024 * 1024

_F32 = jnp.float32
_BF16 = jnp.bfloat16
_INT_MIN = -(2 ** 31)
_NT = (((1,), (1,)), ((), ()))


def _rmsnorm(x, g):
    return x * lax.rsqrt(jnp.mean(x * x, axis=-1, keepdims=True) + EPS) * g


def _swap_halves(x):
    w = x.shape[-1]
    lane = lax.broadcasted_iota(jnp.int32, x.shape, 1)
    fwd = pltpu.roll(x, w - HEAD_DIM // 2, 1)
    bwd = pltpu.roll(x, HEAD_DIM // 2, 1)
    return jnp.where((lane & (HEAD_DIM // 2)) == 0, fwd, bwd)


def _rope(x, cos, sin):
    return x * cos + _swap_halves(x) * sin


def _tile_lanes(t, n):
    return jnp.concatenate([t] * n, axis=1) if n > 1 else t


def _project(x, cos, sin, g_ref, win_ref, width):
    hn = _rmsnorm(x, g_ref[...]).astype(_BF16)
    z = jnp.dot(hn, win_ref[...], preferred_element_type=_F32)
    w = width
    n = w // LANES
    cos_w, sin_w = _tile_lanes(cos, n), _tile_lanes(sin, n)
    out = {
        "xl": z[:, 0:w],
        "gate": z[:, w:2 * w],
        "q": _rope(z[:, 2 * w:3 * w], cos_w, sin_w),
        "k": _rope(z[:, 3 * w:4 * w], cos_w, sin_w),
        "v": z[:, 4 * w:5 * w],
        "qi": _rope(z[:, 5 * w:6 * w], cos_w, sin_w),
        "kw": z[:, 6 * w:6 * w + LANES],
    }
    out["kir"] = _rope(out["kw"], cos, sin)
    return out


def _expm1(x):
    u = jnp.exp(x)
    d = u - 1.0
    return jnp.where(u == 1.0, x, jnp.where(d == -1.0, -1.0, d * x / jnp.log(u)))


def _lru_gates(xc, wg_ref, ba, bi, lam):
    w = xc.shape[1]
    g = jnp.dot(xc.astype(_BF16), wg_ref[...], preferred_element_type=_F32)
    r = jax.nn.sigmoid(g[:, :w] + ba)
    i = jax.nn.sigmoid(g[:, w:] + bi)
    log_a = (-RG_C * jax.nn.softplus(-lam)) * r
    a = jnp.exp(log_a)
    u = jnp.sqrt(-_expm1(2.0 * log_a)) * (i * xc)
    return a, u


def _scan_rows(a, u):
    tm, w = a.shape
    row = lax.broadcasted_iota(jnp.int32, a.shape, 0)
    s = 1
    while s < tm:
        if s < SUBLANES:
            keep = row >= s
            a_sh = jnp.where(keep, pltpu.roll(a, s, 0), 1.0)
            u_sh = jnp.where(keep, pltpu.roll(u, s, 0), 0.0)
        else:
            a_sh = jnp.concatenate([jnp.ones((s, w), _F32), a[:tm - s]], axis=0)
            u_sh = jnp.concatenate([jnp.zeros((s, w), _F32), u[:tm - s]], axis=0)
        u = a * u_sh + u
        a = a * a_sh
        s *= 2
    return a, u


def _prompt_proj_kernel(x_ref, cos_ref, sin_ref, g_ref, win_ref, convw_ref, convb_ref, wg_ref,
                        ba_ref, bi_ref, lam_ref,
                        ktf_ref, vtf_ref, kir_ref, kw_ref, q_ref, qi_ref, kt_ref, kit_ref, vb_ref,
                        lru_ref, h_ref, conv_ref, hcar_ref, ccar_ref, *, width):
    t = pl.program_id(1)

    @pl.when(t == 0)
    def _():
        hcar_ref[...] = jnp.zeros_like(hcar_ref)
        ccar_ref[...] = jnp.zeros_like(ccar_ref)

    x = x_ref[0]
    tm = x.shape[0]
    p = _project(x, cos_ref[...], sin_ref[...], g_ref, win_ref, width)
    scale = HEAD_DIM ** -0.5
    kt = p["k"].T
    ktf_ref[0] = kt
    vtf_ref[0] = p["v"].T
    kir_ref[0] = p["kir"]
    kw_ref[0] = p["kw"]
    q_ref[0] = (p["q"] * scale).astype(_BF16)
    qi_ref[0] = (p["qi"] * (IDX_DIM ** -0.5)).astype(_BF16)
    kt_ref[0] = kt.astype(_BF16)
    ki = p["kir"][:, :IDX_DIM]
    kit_ref[0] = jnp.concatenate([ki, ki], axis=1).T.astype(_BF16)
    vb_ref[0] = p["v"].astype(_BF16)

    xl = p["xl"]
    ext = jnp.concatenate([ccar_ref[...], xl], axis=0)
    xc = convb_ref[...]
    for j in range(CONV_W):
        lo = SUBLANES - (CONV_W - 1) + j
        xc = xc + ext[lo:lo + tm] * convw_ref[j:j + 1, :]
    ccar_ref[...] = xl[tm - SUBLANES:]
    conv_ref[0] = xl[tm - SUBLANES:]

    a, u = _lru_gates(xc, wg_ref, ba_ref[...], bi_ref[...], lam_ref[...])
    a_cum, u_cum = _scan_rows(a, u)
    hs = a_cum * hcar_ref[0:1, :] + u_cum
    h_last = jnp.broadcast_to(hs[tm - 1:tm, :], (SUBLANES, width))
    hcar_ref[...] = h_last
    h_ref[0] = h_last
    lru_ref[0] = (hs * jax.nn.gelu(p["gate"])).astype(_BF16)


def _prompt_proj(x, cos, sin, g_pre, w_in, conv_w, conv_b, w_gate, b_a, b_i, lam, *, tile):
    b, t, d = x.shape
    w = conv_b.shape[-1]
    nt = t // tile
    const = lambda shape: pl.BlockSpec(shape, lambda bi, ti: (0,) * len(shape))
    tok = lambda last: pl.BlockSpec((1, tile, last), lambda bi, ti: (bi, ti, 0))
    state = pl.BlockSpec((1, SUBLANES, w), lambda bi, ti: (bi, 0, 0))
    out_shape = (
        jax.ShapeDtypeStruct((b, w, t), _F32),
        jax.ShapeDtypeStruct((b, w, t), _F32),
        jax.ShapeDtypeStruct((b, t, LANES), _F32),
        jax.ShapeDtypeStruct((b, t, LANES), _F32),
        jax.ShapeDtypeStruct((b, t, w), _BF16),
        jax.ShapeDtypeStruct((b, t, w), _BF16),
        jax.ShapeDtypeStruct((b, w, t), _BF16),
        jax.ShapeDtypeStruct((b, LANES, t), _BF16),
        jax.ShapeDtypeStruct((b, t, w), _BF16),
        jax.ShapeDtypeStruct((b, t, w), _BF16),
        jax.ShapeDtypeStruct((b, SUBLANES, w), _F32),
        jax.ShapeDtypeStruct((b, SUBLANES, w), _F32),
    )
    out_specs = (
        pl.BlockSpec((1, w, tile), lambda bi, ti: (bi, 0, ti)),
        pl.BlockSpec((1, w, tile), lambda bi, ti: (bi, 0, ti)),
        tok(LANES), tok(LANES), tok(w), tok(w),
        pl.BlockSpec((1, w, tile), lambda bi, ti: (bi, 0, ti)),
        pl.BlockSpec((1, LANES, tile), lambda bi, ti: (bi, 0, ti)),
        tok(w), tok(w), state, state,
    )
    in_specs = [
        tok(d),
        pl.BlockSpec((tile, LANES), lambda bi, ti: (ti, 0)),
        pl.BlockSpec((tile, LANES), lambda bi, ti: (ti, 0)),
        const((1, d)), const(w_in.shape), const(conv_w.shape), const((1, w)), const(w_gate.shape),
        const((1, w)), const((1, w)), const((1, w)),
    ]
    return pl.pallas_call(
        functools.partial(_prompt_proj_kernel, width=w),
        grid=(b, nt),
        in_specs=in_specs,
        out_specs=out_specs,
        out_shape=out_shape,
        scratch_shapes=[pltpu.VMEM((SUBLANES, w), _F32), pltpu.VMEM((SUBLANES, w), _F32)],
        compiler_params=pltpu.CompilerParams(
            dimension_semantics=("arbitrary", "arbitrary"), vmem_limit_bytes=VMEM_LIMIT_BYTES),
        name="prompt_proj",
    )(x, cos, sin, g_pre, w_in, conv_w, conv_b, w_gate, b_a, b_i, lam)


def _sortable_key(score, admissible):
    score = jnp.where(score == 0.0, 0.0, score)
    bits = pltpu.bitcast(score, jnp.int32)
    key = bits ^ ((bits >> 31) & 0x7FFFFFFF)
    return jnp.where(admissible, key, _INT_MIN)


def _topk_bias(key_ref, bias_ref, sel_ref, *, ncols, topk, unroll_rows):
    rows = key_ref.shape[0]
    nchunk = ncols // LANES
    kf = float(topk)

    def count(r0, nr, pred):
        acc = jnp.zeros((nr, LANES), _F32)
        for c in range(nchunk):
            acc = acc + pred(key_ref[r0:r0 + nr, c * LANES:(c + 1) * LANES], c)
        return jnp.broadcast_to(jnp.sum(acc, axis=1, keepdims=True), (nr, LANES))

    def thr_step(r0, nr, bit, thr):
        cand = thr + bit
        cnt = count(r0, nr, lambda k, c: jnp.where(k >= cand, 1.0, 0.0))
        return jnp.where(cnt >= kf, cand, thr)

    group = SUBLANES if unroll_rows else rows
    for r0 in range(0, rows, group):
        thr = jnp.full((group, LANES), _INT_MIN, jnp.int32)
        if unroll_rows:
            for b in range(31, -1, -1):
                thr = thr_step(r0, group, jnp.int32(_INT_MIN if b == 31 else 1 << b), thr)
        else:
            thr = lax.fori_loop(
                0, 32, lambda i, t: thr_step(r0, group, jnp.left_shift(jnp.int32(1), 31 - i), t), thr)
        sel_ref[0, r0:r0 + group, :] = thr

    thr = sel_ref[0]
    lane = lax.broadcasted_iota(jnp.int32, (rows, LANES), 1)
    need = kf - count(0, rows, lambda k, c: jnp.where(k > thr, 1.0, 0.0))
    n_eq = count(0, rows, lambda k, c: jnp.where(k == thr, 1.0, 0.0))
    sel_ref[1] = jnp.full((rows, LANES), ncols, jnp.int32)

    @pl.when(jnp.max(n_eq - need) > 0.0)
    def _():
        nbits = max(1, (ncols - 1).bit_length())

        def cut_step(i, cut):
            cand = cut + jnp.left_shift(jnp.int32(1), nbits - 1 - i)
            below = count(0, rows, lambda k, c: jnp.where(
                k == thr, jnp.where(c * LANES + lane < cand, 1.0, 0.0), 0.0))
            return jnp.where(below < need, cand, cut)

        cut = lax.fori_loop(0, nbits, cut_step, jnp.zeros((rows, LANES), jnp.int32))
        sel_ref[1] = cut + 1

    jcut = jnp.where(thr == _INT_MIN, 0, sel_ref[1])
    for c in range(nchunk):
        k = key_ref[:, c * LANES:(c + 1) * LANES]
        tie = jnp.where(c * LANES + lane < jcut, 0.0, -jnp.inf)
        bias_ref[:, c * LANES:(c + 1) * LANES] = jnp.where(
            k > thr, 0.0, jnp.where(k == thr, tie, -jnp.inf))


def _prompt_attn_kernel(qi_ref, q_ref, kw_ref, kit_ref, kt_ref, v_ref, o_ref,
                        key_ref, bias_ref, sel_ref, wib_ref, *, ncols, jbase, topk):
    j = jbase + pl.program_id(1)
    qb = q_ref.shape[1]
    width = q_ref.shape[2]
    heads = width // HEAD_DIM
    lane_w = lax.broadcasted_iota(jnp.int32, (qb, width), 1)
    even = (lane_w & HEAD_DIM) == 0

    kw = kw_ref[0]
    for h in range(IDX_HEADS):
        col = IDX_DIM + h
        wib_ref[h] = jnp.broadcast_to(kw[:, col:col + 1], (qb, LANES))

    qi = qi_ref[0]
    zero = jnp.zeros_like(qi)
    qi_par = (jnp.where(even, qi, zero), jnp.where(even, zero, qi))
    cw = 2 * LANES
    qpos = j * qb + lax.broadcasted_iota(jnp.int32, (qb, cw), 0)
    for c in range(ncols // cw):
        kic = kit_ref[0, :, c * cw:(c + 1) * cw]
        acc = jnp.zeros((qb, cw), _F32)
        for h in range(IDX_HEADS):
            lhs = qi_par[h % 2][:, (h // 2) * LANES:(h // 2 + 1) * LANES]
            d = jnp.dot(lhs, kic, preferred_element_type=_F32)
            acc = acc + jnp.maximum(d, 0.0) * _tile_lanes(wib_ref[h], cw // LANES)
        score = acc * (IDX_HEADS ** -0.5)
        col = c * cw + lax.broadcasted_iota(jnp.int32, (qb, cw), 1)
        key_ref[:, c * cw:(c + 1) * cw] = _sortable_key(score, col <= qpos)

    _topk_bias(key_ref, bias_ref, sel_ref, ncols=ncols, topk=topk, unroll_rows=True)

    q = q_ref[0]
    zero = jnp.zeros_like(q)
    q_par = (jnp.where(even, q, zero), jnp.where(even, zero, q))
    lane = lax.broadcasted_iota(jnp.int32, (qb, LANES), 1)
    for g in range(heads // 2):
        sl = slice(g * LANES, (g + 1) * LANES)
        kt = kt_ref[0, sl, :]
        vv = v_ref[0, :, sl]
        outs = []
        for par in range(2):
            s = jnp.dot(q_par[par][:, sl], kt, preferred_element_type=_F32) + bias_ref[...]
            m = jnp.max(s, axis=1, keepdims=True)
            p = jnp.exp(s - m)
            l = jnp.sum(p, axis=1, keepdims=True)
            o = jnp.dot(p.astype(_BF16), vv, preferred_element_type=_F32)
            outs.append(o * (1.0 / l))
        o_ref[0, :, sl] = jnp.where(lane < HEAD_DIM, outs[0], outs[1]).astype(o_ref.dtype)


def _prompt_attn(qi, q, kw, kit, kt, vb, *, ncols, jbase, nblocks, topk):
    b, t, w = q.shape
    qblk = lambda last: pl.BlockSpec((1, QBLOCK, last), lambda bi, ji: (bi, jbase + ji, 0))
    return pl.pallas_call(
        functools.partial(_prompt_attn_kernel, ncols=ncols, jbase=jbase, topk=topk),
        grid=(b, nblocks),
        in_specs=[
            qblk(w), qblk(w), qblk(LANES),
            pl.BlockSpec((1, LANES, ncols), lambda bi, ji: (bi, 0, 0)),
            pl.BlockSpec((1, w, ncols), lambda bi, ji: (bi, 0, 0)),
            pl.BlockSpec((1, ncols, w), lambda bi, ji: (bi, 0, 0)),
        ],
        out_specs=pl.BlockSpec((1, QBLOCK, w), lambda bi, ji: (bi, ji, 0)),
        out_shape=jax.ShapeDtypeStruct((b, nblocks * QBLOCK, w), _BF16),
        scratch_shapes=[
            pltpu.VMEM((QBLOCK, ncols), jnp.int32),
            pltpu.VMEM((QBLOCK, ncols), _F32),
            pltpu.VMEM((2, QBLOCK, LANES), jnp.int32),
            pltpu.VMEM((IDX_HEADS, QBLOCK, LANES), _F32),
        ],
        compiler_params=pltpu.CompilerParams(
            dimension_semantics=("arbitrary", "arbitrary"), vmem_limit_bytes=VMEM_LIMIT_BYTES),
        name=f"prompt_attn_{ncols}",
    )(qi, q, kw, kit, kt, vb)


def _mlp_kernel(x_ref, lru_ref, att_ref, wout_ref, gpm_ref, gpf_ref, w1_ref, w2_ref, gpo_ref, y_ref):
    w = lru_ref.shape[1]
    mix = (jnp.dot(lru_ref[...], wout_ref[0:w, :], preferred_element_type=_F32)
           + jnp.dot(att_ref[...], wout_ref[w:, :], preferred_element_type=_F32))
    x1 = x_ref[...] + _rmsnorm(mix, gpm_ref[...])
    hn = _rmsnorm(x1, gpf_ref[...]).astype(_BF16)
    f = jnp.dot(hn, w1_ref[...], preferred_element_type=_F32)
    f = jnp.square(jnp.maximum(f, 0.0)).astype(_BF16)
    f2 = jnp.dot(f, w2_ref[...], preferred_element_type=_F32)
    y_ref[...] = x1 + _rmsnorm(f2, gpo_ref[...])


def _mlp(x, lru, att, w_out, g_post_mix, g_pre_ffn, w_ff1, w_ff2, g_post_ffn, *, tile):
    n, d = x.shape
    w = lru.shape[1]
    const = lambda shape: pl.BlockSpec(shape, lambda i: (0,) * len(shape),
                                       pipeline_mode=pl.Buffered(1))
    tok = lambda last: pl.BlockSpec((tile, last), lambda i: (i, 0))
    return pl.pallas_call(
        _mlp_kernel,
        grid=(n // tile,),
        in_specs=[tok(d), tok(w), tok(w), const(w_out.shape), const((1, d)), const((1, d)),
                  const(w_ff1.shape), const(w_ff2.shape), const((1, d))],
        out_specs=tok(d),
        out_shape=jax.ShapeDtypeStruct((n, d), _F32),
        compiler_params=pltpu.CompilerParams(
            dimension_semantics=("arbitrary",), vmem_limit_bytes=VMEM_LIMIT_BYTES),
        name="mlp",
    )(x, lru, att, w_out, g_post_mix, g_pre_ffn, w_ff1, w_ff2, g_post_ffn)


def _sample_proj_kernel(x_ref, cos_ref, sin_ref, g_ref, win_ref, convw_ref, convb_ref, wg_ref,
                        ba_ref, bi_ref, lam_ref, cst_ref, h0_ref,
                        k_ref, v_ref, kir_ref, kw_ref, q_ref, qi_ref, lru_ref, h_ref, cnew_ref,
                        *, width, nseq, nstep):
    p = _project(x_ref[...], cos_ref[...], sin_ref[...], g_ref, win_ref, width)
    k_ref[...] = p["k"]
    v_ref[...] = p["v"]
    kir_ref[...] = p["kir"]
    kw_ref[...] = p["kw"]
    q_ref[...] = (p["q"] * (HEAD_DIM ** -0.5)).astype(_BF16)
    qi_ref[...] = (p["qi"] * (IDX_DIM ** -0.5)).astype(_BF16)

    xl = p["xl"]
    n = nseq * nstep
    ext = jnp.concatenate([cst_ref[...], xl], axis=0)
    xc = convb_ref[...]
    for j in range(CONV_W):
        xc = xc + ext[j * nseq:j * nseq + n] * convw_ref[j:j + 1, :]
    cnew_ref[...] = ext[n:]

    a, u = _lru_gates(xc, wg_ref, ba_ref[...], bi_ref[...], lam_ref[...])
    h = h0_ref[...]
    hs = []
    for s in range(nstep):
        h = a[s * nseq:(s + 1) * nseq] * h + u[s * nseq:(s + 1) * nseq]
        hs.append(h)
    h_ref[...] = h
    lru_ref[...] = (jnp.concatenate(hs, axis=0) * jax.nn.gelu(p["gate"])).astype(_BF16)


def _sample_proj(x, cos, sin, g_pre, w_in, conv_w, conv_b, w_gate, b_a, b_i, lam, cst, h0,
                 *, nseq, nstep):
    n, d = x.shape
    w = conv_b.shape[-1]
    out_shape = (
        jax.ShapeDtypeStruct((n, w), _F32), jax.ShapeDtypeStruct((n, w), _F32),
        jax.ShapeDtypeStruct((n, LANES), _F32), jax.ShapeDtypeStruct((n, LANES), _F32),
        jax.ShapeDtypeStruct((n, w), _BF16), jax.ShapeDtypeStruct((n, w), _BF16),
        jax.ShapeDtypeStruct((n, w), _BF16),
        jax.ShapeDtypeStruct((nseq, w), _F32),
        jax.ShapeDtypeStruct(((CONV_W - 1) * nseq, w), _F32),
    )
    return pl.pallas_call(
        functools.partial(_sample_proj_kernel, width=w, nseq=nseq, nstep=nstep),
        out_shape=out_shape,
        compiler_params=pltpu.CompilerParams(vmem_limit_bytes=VMEM_LIMIT_BYTES),
        name="sample_proj",
    )(x, cos, sin, g_pre, w_in, conv_w, conv_b, w_gate, b_a, b_i, lam, cst, h0)


def _sample_index_kernel(pt_ref, qi_ref, wi_ref, kin_ref, cache_ref, sc_ref, buf_ref, sem_ref,
                         *, npages, page, nstep):
    b = pl.program_id(0)
    nb = pl.num_programs(0)
    slot = b % 2

    def page_copy(seq, p, sl):
        return pltpu.make_async_copy(cache_ref.at[pt_ref[seq, p]],
                                     buf_ref.at[sl, :, pl.ds(p * page, page)], sem_ref.at[sl])

    @pl.when(b == 0)
    def _():
        for p in range(npages):
            page_copy(0, p, 0).start()

    @pl.when(b + 1 < nb)
    def _():
        for p in range(npages):
            page_copy(b + 1, p, 1 - slot).start()

    for p in range(npages):
        page_copy(b, p, slot).wait()

    qi = qi_ref[0]
    wi = wi_ref[0]

    def scores(keys_t):
        n = keys_t.shape[1]
        d = jnp.dot(qi, keys_t.astype(_BF16), preferred_element_type=_F32)
        wd = jnp.maximum(d, 0.0) * wi
        return jnp.sum(wd.reshape(nstep, IDX_HEADS, n), axis=1) * (IDX_HEADS ** -0.5)

    past = npages * page
    chunk = 16 * page
    for c in range(past // chunk):
        sc_ref[0, :, c * chunk:(c + 1) * chunk] = scores(buf_ref[slot, :, c * chunk:(c + 1) * chunk])
    sc_ref[0, :, past:] = scores(kin_ref[0])


def _sample_index(page_table, qi_rows, wi_rows, ki_new, cache_kidx, *, nstep):
    nseq, npages = page_table.shape
    page = cache_kidx.shape[2]
    past = npages * page
    ncols = past + LANES
    grid_spec = pltpu.PrefetchScalarGridSpec(
        num_scalar_prefetch=1,
        grid=(nseq,),
        in_specs=[
            pl.BlockSpec((1,) + qi_rows.shape[1:], lambda b, pt: (b, 0, 0)),
            pl.BlockSpec((1,) + wi_rows.shape[1:], lambda b, pt: (b, 0, 0)),
            pl.BlockSpec((1,) + ki_new.shape[1:], lambda b, pt: (b, 0, 0)),
            pl.BlockSpec(memory_space=pl.ANY),
        ],
        out_specs=pl.BlockSpec((1, nstep, ncols), lambda b, pt: (b, 0, 0)),
        scratch_shapes=[pltpu.VMEM((2, IDX_DIM, past), _F32), pltpu.SemaphoreType.DMA((2,))],
    )
    return pl.pallas_call(
        functools.partial(_sample_index_kernel, npages=npages, page=page, nstep=nstep),
        grid_spec=grid_spec,
        out_shape=jax.ShapeDtypeStruct((nseq, nstep, ncols), _F32),
        compiler_params=pltpu.CompilerParams(
            dimension_semantics=("arbitrary",), vmem_limit_bytes=VMEM_LIMIT_BYTES),
        name="sample_index",
    )(page_table, qi_rows, wi_rows, ki_new, cache_kidx)


def _sample_topk_kernel(sc_ref, bias_ref, key_ref, sel_ref, *, past, nstep, topk):
    rows, ncols = sc_ref.shape
    cw = 4 * LANES
    step = lax.broadcasted_iota(jnp.int32, (rows, cw), 0) % nstep
    for c in range(ncols // cw + (1 if ncols % cw else 0)):
        lo = c * cw
        wdt = min(cw, ncols - lo)
        col = lo + lax.broadcasted_iota(jnp.int32, (rows, wdt), 1)
        key_ref[:, lo:lo + wdt] = _sortable_key(sc_ref[:, lo:lo + wdt], col <= past + step[:, :wdt])
    _topk_bias(key_ref, bias_ref, sel_ref, ncols=ncols, topk=topk, unroll_rows=False)


def _sample_topk(sc, *, past, nstep, topk):
    rows, ncols = sc.shape
    return pl.pallas_call(
        functools.partial(_sample_topk_kernel, past=past, nstep=nstep, topk=topk),
        out_shape=jax.ShapeDtypeStruct((rows, ncols), _F32),
        scratch_shapes=[pltpu.VMEM((rows, ncols), jnp.int32), pltpu.VMEM((2, rows, LANES), jnp.int32)],
        compiler_params=pltpu.CompilerParams(vmem_limit_bytes=VMEM_LIMIT_BYTES),
        name="sample_topk",
    )(sc)


def _sample_attn_kernel(pt_ref, q_ref, bias_ref, btail_ref, kn_ref, vn_ref, ck_ref, cv_ref, o_ref,
                        kbuf_ref, vbuf_ref, sem_ref, qbd_ref, m_ref, l_ref, acc_ref,
                        *, cpages, page, nstep):
    b = pl.program_id(0)
    c = pl.program_id(1)
    nchunks = pl.num_programs(1)
    step = b * nchunks + c
    total = pl.num_programs(0) * nchunks
    slot = step % 2
    width = q_ref.shape[2]
    heads = width // HEAD_DIM
    rows = nstep * heads

    def copies(seq, chunk, p, sl):
        phys = pt_ref[seq, chunk * cpages + p]
        dst = pl.ds(p * page, page)
        return (pltpu.make_async_copy(ck_ref.at[phys], kbuf_ref.at[sl, :, dst], sem_ref.at[0, sl]),
                pltpu.make_async_copy(cv_ref.at[phys], vbuf_ref.at[sl, :, dst], sem_ref.at[1, sl]))

    def fetch(seq, chunk, sl):
        for p in range(cpages):
            for cp in copies(seq, chunk, p, sl):
                cp.start()

    @pl.when(step == 0)
    def _():
        fetch(0, 0, 0)

    @pl.when(step + 1 < total)
    def _():
        nxt = step + 1
        fetch(nxt // nchunks, nxt % nchunks, 1 - slot)

    for p in range(cpages):
        for cp in copies(b, c, p, slot):
            cp.wait()

    row = lax.broadcasted_iota(jnp.int32, (rows, width), 0)
    lane = lax.broadcasted_iota(jnp.int32, (rows, width), 1)
    own_head = (row % heads) == (lane // HEAD_DIM)

    def expand(x):
        return jnp.broadcast_to(x[:, None, :], (nstep, heads, x.shape[1])).reshape(rows, x.shape[1])

    @pl.when(c == 0)
    def _():
        q = expand(q_ref[0].astype(_F32))
        qbd_ref[...] = jnp.where(own_head, q, 0.0).astype(_BF16)
        m_ref[...] = jnp.full(m_ref.shape, -1e30, _F32)
        l_ref[...] = jnp.zeros_like(l_ref)
        acc_ref[...] = jnp.zeros_like(acc_ref)

    s = jnp.dot(qbd_ref[...], kbuf_ref[slot].astype(_BF16), preferred_element_type=_F32)
    s = s + expand(bias_ref[0])
    m_old = m_ref[...]
    m_new = jnp.maximum(m_old, jnp.max(s, axis=1, keepdims=True))
    alpha = jnp.exp(m_old - m_new)
    pr = jnp.exp(s - m_new)
    l_ref[...] = alpha * l_ref[...] + jnp.sum(pr, axis=1, keepdims=True)
    acc_ref[...] = alpha * acc_ref[...] + lax.dot_general(
        pr.astype(_BF16), vbuf_ref[slot].astype(_BF16), _NT, preferred_element_type=_F32)
    m_ref[...] = m_new

    @pl.when(c == nchunks - 1)
    def _():
        qf = qbd_ref[...].astype(_F32)
        kn = kn_ref[0].astype(_BF16).astype(_F32)
        vn = vn_ref[0].astype(_BF16).astype(_F32)
        bt = expand(btail_ref[0])
        sn = [jnp.sum(qf * kn[t:t + 1, :], axis=1, keepdims=True) + bt[:, t:t + 1]
              for t in range(nstep)]
        m_old = m_ref[...]
        m_new = m_old
        for t in range(nstep):
            m_new = jnp.maximum(m_new, sn[t])
        alpha = jnp.exp(m_old - m_new)
        l = alpha * l_ref[...]
        acc = alpha * acc_ref[...]
        for t in range(nstep):
            pt_ = jnp.exp(sn[t] - m_new)
            l = l + pt_
            acc = acc + pt_.astype(_BF16).astype(_F32) * vn[t:t + 1, :]
        o = jnp.where(own_head, acc * (1.0 / l), 0.0)
        o_ref[0] = jnp.sum(o.reshape(nstep, heads, width), axis=1).astype(o_ref.dtype)


def _sample_attn(page_table, q, bias, k_new, v_new, cache_k, cache_v, *, cpages):
    nseq, nstep, w = q.shape
    npages = page_table.shape[1]
    page = cache_k.shape[2]
    chunk = cpages * page
    nchunks = npages // cpages
    rows = nstep * (w // HEAD_DIM)
    tail_blk = (npages * page) // LANES
    grid_spec = pltpu.PrefetchScalarGridSpec(
        num_scalar_prefetch=1,
        grid=(nseq, nchunks),
        in_specs=[
            pl.BlockSpec((1, nstep, w), lambda b, c, pt: (b, 0, 0)),
            pl.BlockSpec((1, nstep, chunk), lambda b, c, pt: (b, 0, c)),
            pl.BlockSpec((1, nstep, LANES), lambda b, c, pt: (b, 0, tail_blk)),
            pl.BlockSpec((1, nstep, w), lambda b, c, pt: (b, 0, 0)),
            pl.BlockSpec((1, nstep, w), lambda b, c, pt: (b, 0, 0)),
            pl.BlockSpec(memory_space=pl.ANY),
            pl.BlockSpec(memory_space=pl.ANY),
        ],
        out_specs=pl.BlockSpec((1, nstep, w), lambda b, c, pt: (b, 0, 0)),
        scratch_shapes=[
            pltpu.VMEM((2, w, chunk), _F32), pltpu.VMEM((2, w, chunk), _F32),
            pltpu.SemaphoreType.DMA((2, 2)),
            pltpu.VMEM((rows, w), _BF16),
            pltpu.VMEM((rows, 1), _F32), pltpu.VMEM((rows, 1), _F32), pltpu.VMEM((rows, w), _F32),
        ],
    )
    return pl.pallas_call(
        functools.partial(_sample_attn_kernel, cpages=cpages, page=page, nstep=nstep),
        grid_spec=grid_spec,
        out_shape=jax.ShapeDtypeStruct((nseq, nstep, w), _BF16),
        compiler_params=pltpu.CompilerParams(
            dimension_semantics=("arbitrary", "arbitrary"), vmem_limit_bytes=VMEM_LIMIT_BYTES),
        name="sample_attn",
    )(page_table, q, bias, bias, k_new, v_new, cache_k, cache_v)


def _rope_tables(pos):
    half = HEAD_DIM // 2
    freq = ROPE_THETA ** (-jnp.arange(half, dtype=_F32) / half)
    ang = pos.astype(_F32)[:, None] * freq[None, :]
    cos, sin = jnp.cos(ang), jnp.sin(ang)
    reps = LANES // HEAD_DIM
    return (jnp.tile(jnp.concatenate([cos, cos], axis=1), (1, reps)),
            jnp.tile(jnp.concatenate([-sin, sin], axis=1), (1, reps)))


def _block_diag(w):
    nb, bw, _ = w.shape
    eye = jnp.eye(nb, dtype=w.dtype)
    return (w[:, :, None, :] * eye[:, None, :, None]).reshape(nb * bw, nb * bw)


def kernel(x_prompt, x_sample, cache_k, cache_v, cache_kidx, state_h, state_conv, page_table,
           g_pre_mix, w_in, conv_w, conv_b, w_a, b_a, w_i, b_i, lam, w_out, g_post_mix,
           g_pre_ffn, w_ff1, w_ff2, g_post_ffn):
    nb, seq, d = x_prompt.shape
    nseq, nstep, _ = x_sample.shape
    depth = g_pre_mix.shape[0]
    w = conv_b.shape[-1]
    heads = cache_k.shape[-2]
    page = cache_k.shape[2]
    past = page_table.shape[1] * page
    assert heads * HEAD_DIM == w and w % LANES == 0

    cos_p, sin_p = _rope_tables(jnp.arange(seq))
    cos_s, sin_s = _rope_tables(past + jnp.arange(nstep))
    cos_s, sin_s = jnp.repeat(cos_s, nseq, axis=0), jnp.repeat(sin_s, nseq, axis=0)

    xp = x_prompt
    xs = x_sample
    outs_p, outs_s = [], []
    for l in range(depth):
        row = lambda v: v[l].reshape(1, -1)
        main = 6 * w
        tail = w_in[l][:, main:]
        w_in_l = jnp.concatenate(
            [w_in[l][:, :main], tail, jnp.zeros((d, LANES - tail.shape[1]), w_in.dtype)],
            axis=1).astype(_BF16)
        w_gate = jnp.concatenate([_block_diag(w_a[l]), _block_diag(w_i[l])], axis=1).astype(_BF16)
        proj_w = (row(g_pre_mix), w_in_l, conv_w[l], row(conv_b), w_gate, row(b_a), row(b_i), row(lam))
        mlp_w = (w_out[l].astype(_BF16), row(g_post_mix), row(g_pre_ffn),
                 w_ff1[l].astype(_BF16), w_ff2[l].astype(_BF16), row(g_post_ffn))

        (ktf_p, vtf_p, kir_p, kw_p, q_p, qi_p, kt_p, kit_p, vb_p, lru_p, h_p, conv_p) = _prompt_proj(
            xp, cos_p, sin_p, *proj_w, tile=256)
        topk_p = min(TOPK_MAX, seq // 4)
        nqb = seq // QBLOCK
        group = 2
        att = []
        for jb in range(0, nqb, group):
            nblk = min(group, nqb - jb)
            att.append(_prompt_attn(qi_p, q_p, kw_p, kit_p, kt_p, vb_p,
                                    ncols=(jb + nblk) * QBLOCK, jbase=jb, nblocks=nblk, topk=topk_p))
        att_p = jnp.concatenate(att, axis=1)
        xp = _mlp(xp.reshape(nb * seq, d), lru_p.reshape(nb * seq, w), att_p.reshape(nb * seq, w),
                  *mlp_w, tile=256).reshape(nb, seq, d)
        from_t = lambda v: jnp.transpose(v.reshape(nb, heads, HEAD_DIM, seq), (0, 3, 1, 2))
        outs_p.append((from_t(ktf_p), from_t(vtf_p),
                       kir_p[:, :, :IDX_DIM], h_p[:, 0, :], conv_p[:, SUBLANES - (CONV_W - 1):, :]))

        n_s = nseq * nstep
        to_steps = lambda v: jnp.swapaxes(v, 0, 1).reshape((-1,) + v.shape[2:])
        to_seqs = lambda v: jnp.swapaxes(v.reshape((-1, nseq) + v.shape[1:]), 0, 1)
        (k_s, v_s, kir_s, kw_s, q_s, qi_s, lru_s, h_s, conv_s) = _sample_proj(
            to_steps(xs), cos_s, sin_s, *proj_w, to_steps(state_conv[l]), state_h[l],
            nseq=nseq, nstep=nstep)
        k_s, v_s, kir_s, kw_s, q_s, qi_s, lru_s = map(to_seqs, (k_s, v_s, kir_s, kw_s, q_s, qi_s, lru_s))
        ki_s = kir_s[:, :, :IDX_DIM]
        ki_new = jnp.swapaxes(jnp.concatenate(
            [ki_s, jnp.zeros((nseq, LANES - nstep, IDX_DIM), _F32)], axis=1), 1, 2)
        qi_rows = qi_s.reshape(nseq, nstep * IDX_HEADS, IDX_DIM)
        wi_rows = kw_s[:, :, IDX_DIM:IDX_DIM + IDX_HEADS].reshape(nseq, nstep * IDX_HEADS, 1)
        kidx_t = jnp.swapaxes(cache_kidx[l], 1, 2)
        k_t = jnp.transpose(cache_k[l], (0, 2, 3, 1)).reshape(-1, w, page)
        v_t = jnp.transpose(cache_v[l], (0, 2, 3, 1)).reshape(-1, w, page)
        sc = _sample_index(page_table, qi_rows, wi_rows, ki_new, kidx_t, nstep=nstep)
        topk_s = min(TOPK_MAX, (past + nstep) // 4)
        bias = _sample_topk(sc.reshape(n_s, -1), past=past, nstep=nstep, topk=topk_s)
        att_s = _sample_attn(page_table, q_s, bias.reshape(nseq, nstep, -1), k_s, v_s, k_t, v_t,
                             cpages=16)
        xs = _mlp(xs.reshape(n_s, d), lru_s.reshape(n_s, w), att_s.reshape(n_s, w),
                  *mlp_w, tile=n_s).reshape(nseq, nstep, d)
        outs_s.append((k_s.reshape(nseq, nstep, heads, HEAD_DIM),
                       v_s.reshape(nseq, nstep, heads, HEAD_DIM), ki_s, h_s,
                       jnp.swapaxes(conv_s.reshape(CONV_W - 1, nseq, w), 0, 1)))

    stack = lambda outs, i: jnp.stack([o[i] for o in outs])
    return (xp, xs) + tuple(stack(outs_p, i) for i in range(5)) + tuple(stack(outs_s, i) for i in range(5))
```
